```python
import math
import jax, jax.numpy as jnp
from jax import lax
import numpy as np

D_MODEL = 1024
BATCH = 32
SEQ = 2048
DEPTH = 2

N_MIXERS = 2
CONV_WIDTH = 3
HEAD_DIM = 64
N_Q_HEADS = D_MODEL // HEAD_DIM
N_KV_HEADS = 4
GROUP = N_Q_HEADS // N_KV_HEADS
WINDOW = 128
BLOCK = 128
D_FF = ((8 * D_MODEL // 3 + 255) // 256) * 256
QKV_WIDTH = (N_Q_HEADS + 2 * N_KV_HEADS) * HEAD_DIM
N_CONV_LAYERS = (DEPTH + 1) // 2
N_ATTN_LAYERS = DEPTH // 2
EPS = 1e-6

kernel_name = "hybrid_shortconv_swa_sink_alibi_swiglu"


def rmsnorm(x, gain):
    xf = x.astype(jnp.float32)
    r = lax.rsqrt(jnp.mean(xf * xf, axis=-1, keepdims=True) + EPS)
    return (xf * r).astype(x.dtype) * gain


def alibi_slopes():
    h = jnp.arange(1, N_Q_HEADS + 1, dtype=jnp.float32)
    return jnp.exp2(-8.0 * h / N_Q_HEADS)


def short_conv_mixer(h, w_in, conv_w, w_out):
    d = h.shape[-1]
    bcx = h @ w_in
    b_gate, c_gate, xv = jnp.split(bcx, 3, axis=-1)
    u = b_gate * xv
    y = lax.conv_general_dilated(
        u, conv_w[:, None, :].astype(u.dtype),
        window_strides=(1,), padding=[(CONV_WIDTH - 1, 0)],
        dimension_numbers=('NWC', 'WIO', 'NWC'), feature_group_count=d)
    return (c_gate * y) @ w_out


def swa_sink_attention(h, w_qkv, q_gain, k_gain, sinks, w_o):
    bsz, s, _ = h.shape
    nb = s // BLOCK
    qkv = h @ w_qkv
    q_end = N_Q_HEADS * HEAD_DIM
    k_end = q_end + N_KV_HEADS * HEAD_DIM
    q = qkv[..., :q_end].reshape(bsz, s, N_KV_HEADS, GROUP, HEAD_DIM)
    k = qkv[..., q_end:k_end].reshape(bsz, s, N_KV_HEADS, HEAD_DIM)
    v = qkv[..., k_end:].reshape(bsz, s, N_KV_HEADS, HEAD_DIM)
    q = rmsnorm(q, q_gain)
    k = rmsnorm(k, k_gain)

    qb = q.reshape(bsz, nb, BLOCK, N_KV_HEADS, GROUP, HEAD_DIM)

    def band(t):
        tp = jnp.pad(t, ((0, 0), (BLOCK, 0), (0, 0), (0, 0)))
        tb = tp.reshape(bsz, nb + 1, BLOCK, N_KV_HEADS, HEAD_DIM)
        return jnp.concatenate([tb[:, :-1], tb[:, 1:]], axis=2)

    kw = band(k)
    vw = band(v)

    scale = 1.0 / math.sqrt(HEAD_DIM)
    scores = jnp.einsum('bnqkgd,bnskd->bnkgqs', qb, kw).astype(jnp.float32) * scale

    qi = jnp.arange(BLOCK)[:, None]
    kj = jnp.arange(2 * BLOCK)[None, :]
    dist = qi + BLOCK - kj
    key_pos = (jnp.arange(nb) * BLOCK - BLOCK)[:, None, None] + kj[None]
    mask = (dist >= 0)[None] & (dist < WINDOW)[None] & (key_pos >= 0)

    slopes = alibi_slopes().reshape(N_KV_HEADS, GROUP)
    alibi = -slopes[:, :, None, None] * dist.astype(jnp.float32)[None, None]
    scores = scores + alibi[None, None]
    scores = jnp.where(mask[None, :, None, None], scores, -jnp.inf)

    sink = sinks.astype(jnp.float32).reshape(N_KV_HEADS, GROUP)
    sink_col = jnp.broadcast_to(sink[None, None, :, :, None, None],
                                scores.shape[:-1] + (1,))
    logits = jnp.concatenate([scores, sink_col], axis=-1)
    p = jax.nn.softmax(logits, axis=-1)[..., :-1]

    out = jnp.einsum('bnkgqs,bnskd->bnqkgd', p.astype(vw.dtype), vw)
    out = out.reshape(bsz, s, N_Q_HEADS * HEAD_DIM)
    return out @ w_o


def swiglu_ffn(h, w_gate_up, w_down):
    gu = h @ w_gate_up
    g, u = jnp.split(gu, 2, axis=-1)
    return (jax.nn.silu(g) * u) @ w_down


def setup_inputs(seed: int = 0) -> dict:
    key = jax.random.key(seed)
    ks = jax.random.split(key, 14)
    f32 = jnp.float32
    d = D_MODEL

    def w(k, shape, fan_in):
        return jax.random.normal(k, shape, f32) * (fan_in ** -0.5)

    return {
        "x": jax.random.normal(ks[0], (BATCH, SEQ, d), f32),
        "conv_w_in": w(ks[1], (N_CONV_LAYERS, d, 3 * d), d),
        "conv_w": w(ks[2], (N_CONV_LAYERS, CONV_WIDTH, d), CONV_WIDTH),
        "conv_w_out": w(ks[3], (N_CONV_LAYERS, d, d), d),
        "attn_w_qkv": w(ks[4], (N_ATTN_LAYERS, d, QKV_WIDTH), d),
        "attn_q_gain": 1.0 + 0.05 * jax.random.normal(ks[5], (N_ATTN_LAYERS, HEAD_DIM), f32),
        "attn_k_gain": 1.0 + 0.05 * jax.random.normal(ks[6], (N_ATTN_LAYERS, HEAD_DIM), f32),
        "attn_sinks": 0.5 * jax.random.normal(ks[7], (N_ATTN_LAYERS, N_Q_HEADS), f32),
        "attn_w_o": w(ks[8], (N_ATTN_LAYERS, N_Q_HEADS * HEAD_DIM, d), N_Q_HEADS * HEAD_DIM),
        "norm_mixer": 1.0 + 0.05 * jax.random.normal(ks[9], (DEPTH, d), f32),
        "norm_ffn": 1.0 + 0.05 * jax.random.normal(ks[10], (DEPTH, d), f32),
        "ffn_w_gate_up": w(ks[11], (DEPTH, d, 2 * D_FF), d),
        "ffn_w_down": w(ks[12], (DEPTH, D_FF, d), D_FF),
    }


def reference(x, conv_w_in, conv_w, conv_w_out, attn_w_qkv, attn_q_gain, attn_k_gain,
              attn_sinks, attn_w_o, norm_mixer, norm_ffn, ffn_w_gate_up, ffn_w_down):
    for i in range(DEPTH):
        h = rmsnorm(x, norm_mixer[i])
        j = i // N_MIXERS
        if i % N_MIXERS == 0:
            mix = short_conv_mixer(h, conv_w_in[j], conv_w[j], conv_w_out[j])
        else:
            mix = swa_sink_attention(h, attn_w_qkv[j], attn_q_gain[j], attn_k_gain[j],
                                     attn_sinks[j], attn_w_o[j])
        x = x + mix
        h = rmsnorm(x, norm_ffn[i])
        x = x + swiglu_ffn(h, ffn_w_gate_up[i], ffn_w_down[i])
    return x
```

```python
import functools

import numpy as np
import jax
import jax.numpy as jnp
from jax import lax
from jax.experimental import pallas as pl
from jax.experimental.pallas import tpu as pltpu

D_MODEL = 1024
HEAD_DIM = 64
N_Q_HEADS = 16
N_KV_HEADS = 4
GROUP = N_Q_HEADS // N_KV_HEADS
WINDOW = 128
CONV_WIDTH = 3
D_FF = 2816
EPS = 1e-6

LANES = 128
SUBLANES = 8
TOKEN_TILE = 1024
CONV_CHUNK = 256
FFN_CHUNK = 256
VMEM_LIMIT_BYTES = 56 * 1024 * 1024
MASK_VALUE = -1e30

HEAD_PERM = (0, 4, 1, 5, 2, 6, 3, 7, 8, 12, 9, 13, 10, 14, 11, 15)

_F32 = jnp.float32
_BF16 = jnp.bfloat16


def _rmsnorm_bf16(x, gain):
    ms = jnp.mean(x * x, axis=-1, keepdims=True)
    return ((x * lax.rsqrt(ms + EPS)) * gain).astype(_BF16)


def _const_spec(shape):
    return pl.BlockSpec(shape, lambda b, s: (0,) * len(shape),
                        pipeline_mode=pl.Buffered(1))


def _token_spec(tm):
    return pl.BlockSpec((None, tm, D_MODEL), lambda b, s: (b, s, 0))


def _params():
    return pltpu.CompilerParams(
        dimension_semantics=("arbitrary", "arbitrary"),
        vmem_limit_bytes=VMEM_LIMIT_BYTES)


def _conv_kernel(x_ref, g_ref, win_ref, cw_ref, wout_ref, o_ref,
                 h_scr, u_scr, z_scr, *, tm):
    cw = CONV_CHUNK
    @pl.when(pl.program_id(1) == 0)
    def _():
        u_scr[0:SUBLANES, :] = jnp.zeros((SUBLANES, D_MODEL), _F32)

    h_scr[...] = _rmsnorm_bf16(x_ref[...], g_ref[...])
    for j in range(D_MODEL // cw):
        cols = slice(j * cw, (j + 1) * cw)
        bcx = jnp.dot(h_scr[...], win_ref[:, j * 3 * cw:(j + 1) * 3 * cw],
                      preferred_element_type=_F32)
        b_gate = bcx[:, :cw]
        c_gate = bcx[:, cw:2 * cw]
        xv = bcx[:, 2 * cw:]
        u = b_gate * xv
        u_scr[SUBLANES:SUBLANES + tm, cols] = u
        w = cw_ref[:, cols]
        y = (w[2:3] * u
             + w[1:2] * u_scr[SUBLANES - 1:SUBLANES - 1 + tm, cols]
             + w[0:1] * u_scr[SUBLANES - 2:SUBLANES - 2 + tm, cols])
        z_scr[:, cols] = (c_gate * y).astype(_BF16)
        u_scr[0:SUBLANES, cols] = u_scr[tm:tm + SUBLANES, cols]
    mix = jnp.dot(z_scr[...], wout_ref[...], preferred_element_type=_F32)
    o_ref[...] = x_ref[...] + mix


def _conv_mixer(x, gain, w_in, conv_w, w_out, tm):
    bsz, seq, d = x.shape
    kern = functools.partial(_conv_kernel, tm=tm)
    return pl.pallas_call(
        kern,
        grid=(bsz, seq // tm),
        in_specs=[
            _token_spec(tm),
            _const_spec((1, d)),
            _const_spec((d, 3 * d)),
            _const_spec((CONV_WIDTH, d)),
            _const_spec((d, d)),
        ],
        out_specs=_token_spec(tm),
        out_shape=jax.ShapeDtypeStruct(x.shape, _F32),
        scratch_shapes=[
            pltpu.VMEM((tm, d), _BF16),
            pltpu.VMEM((tm + SUBLANES, d), _F32),
            pltpu.VMEM((tm, d), _BF16),
        ],
        compiler_params=_params(),
        name="conv_mixer",
    )(x, gain, w_in, conv_w, w_out)


def _ffn_kernel(x_ref, g_ref, wgu_ref, wd_ref, o_ref, h_scr, a_scr):
    ck = FFN_CHUNK
    h_scr[...] = _rmsnorm_bf16(x_ref[...], g_ref[...])
    for c in range(D_FF // ck):
        gu = jnp.dot(h_scr[...], wgu_ref[:, c * 2 * ck:(c + 1) * 2 * ck],
                     preferred_element_type=_F32)
        g = gu[:, :ck]
        u = gu[:, ck:]
        a_scr[:, c * ck:(c + 1) * ck] = (g * jax.nn.sigmoid(g) * u).astype(_BF16)
    o_ref[...] = x_ref[...] + jnp.dot(a_scr[...], wd_ref[...],
                                      preferred_element_type=_F32)


def _ffn(x, gain, w_gu, w_down, tm):
    bsz, seq, d = x.shape
    return pl.pallas_call(
        _ffn_kernel,
        grid=(bsz, seq // tm),
        in_specs=[
            _token_spec(tm),
            _const_spec((1, d)),
            _const_spec((d, 2 * D_FF)),
            _const_spec((D_FF, d)),
        ],
        out_specs=_token_spec(tm),
        out_shape=jax.ShapeDtypeStruct(x.shape, _F32),
        scratch_shapes=[
            pltpu.VMEM((tm, d), _BF16),
            pltpu.VMEM((tm, D_FF), _BF16),
        ],
        compiler_params=_params(),
        name="swiglu_ffn",
    )(x, gain, w_gu, w_down)


def _pair_rmsnorm(t, gain, scale):
    lane = lax.broadcasted_iota(jnp.int32, t.shape, 1)
    lo = lane < HEAD_DIM
    sq = t * t
    s_lo = jnp.sum(jnp.where(lo, sq, 0.0), axis=-1, keepdims=True)
    s_hi = jnp.sum(jnp.where(lo, 0.0, sq), axis=-1, keepdims=True)
    r = jnp.where(lo, lax.rsqrt(s_lo / HEAD_DIM + EPS), lax.rsqrt(s_hi / HEAD_DIM + EPS))
    out = (t * r) * gain
    if scale != 1.0:
        out = out * scale
    return out.astype(_BF16)


def _attn_kernel(x_ref, g_ref, wqkv_ref, qg_ref, kg_ref, bias_ref, sink_ref, wo_ref,
                 o_ref, q_scr, k_scr, v_scr, a_scr, *, tm):
    blk = WINDOW
    d = D_MODEL
    kvw = N_KV_HEADS * HEAD_DIM
    seq_tile = pl.program_id(1)

    @pl.when(seq_tile == 0)
    def _():
        k_scr[0:blk, :] = jnp.zeros((blk, kvw), _BF16)
        v_scr[0:blk, :] = jnp.zeros((blk, kvw), _BF16)

    h = _rmsnorm_bf16(x_ref[...], g_ref[...])
    for half in range(2):
        q = jnp.dot(h, wqkv_ref[:, half * 512:(half + 1) * 512],
                    preferred_element_type=_F32)
        for j in range(4):
            pb = half * 4 + j
            q_scr[:, pb * LANES:(pb + 1) * LANES] = _pair_rmsnorm(
                q[:, j * LANES:(j + 1) * LANES], qg_ref[...], HEAD_DIM ** -0.5)
    kv = jnp.dot(h, wqkv_ref[:, d:d + 2 * kvw], preferred_element_type=_F32)
    for p in range(2):
        k_scr[blk:blk + tm, p * LANES:(p + 1) * LANES] = _pair_rmsnorm(
            kv[:, p * LANES:(p + 1) * LANES], kg_ref[...], 1.0)
    v_scr[blk:blk + tm, :] = kv[:, kvw:].astype(_BF16)

    lane = lax.broadcasted_iota(jnp.int32, (blk, LANES), 1)
    lo_lane = lane < HEAD_DIM
    rows = 2 * GROUP * blk
    qi = lax.broadcasted_iota(jnp.int32, (rows, blk), 0) % blk
    kc = lax.broadcasted_iota(jnp.int32, (rows, blk), 1)
    in_cur = kc <= qi

    def q_block(qb, carry):
        r0 = pl.multiple_of(qb * blk, blk)
        pen = jnp.where(jnp.logical_and(seq_tile == 0, qb == 0), MASK_VALUE, 0.0)
        for p in range(2):
            parts = []
            for j in range(GROUP):
                qp = q_scr[pl.ds(r0, blk), (p * GROUP + j) * LANES:(p * GROUP + j + 1) * LANES]
                parts.append(jnp.where(lo_lane, qp, jnp.zeros_like(qp)))
                parts.append(jnp.where(lo_lane, jnp.zeros_like(qp), qp))
            qm = jnp.concatenate(parts, axis=0)
            kw = k_scr[pl.ds(r0, 2 * blk), p * LANES:(p + 1) * LANES]
            s2 = lax.dot_general(qm, kw, (((1,), (1,)), ((), ())),
                                 preferred_element_type=_F32)
            s = jnp.where(in_cur, s2[:, blk:], s2[:, :blk] + pen) + bias_ref[p]
            sink = sink_ref[p][:, 0:1]
            m = jnp.maximum(jnp.max(s, axis=-1, keepdims=True), sink)
            e = jnp.exp(s - m)
            denom = jnp.sum(e, axis=-1, keepdims=True) + jnp.exp(sink - m)
            pr = e / denom
            zero = jnp.zeros_like(pr)
            p2 = jnp.concatenate([jnp.where(in_cur, zero, pr),
                                  jnp.where(in_cur, pr, zero)], axis=1).astype(_BF16)
            vw = v_scr[pl.ds(r0, 2 * blk), p * LANES:(p + 1) * LANES]
            o = jnp.dot(p2, vw, preferred_element_type=_F32)
            for j in range(GROUP):
                oj = jnp.where(lo_lane, o[(2 * j) * blk:(2 * j + 1) * blk],
                               o[(2 * j + 1) * blk:(2 * j + 2) * blk])
                a_scr[pl.ds(r0, blk), (p * GROUP + j) * LANES:(p * GROUP + j + 1) * LANES] = (
                    oj.astype(_BF16))
        return carry

    lax.fori_loop(0, tm // blk, q_block, 0)

    k_scr[0:blk, :] = k_scr[tm:tm + blk, :]
    v_scr[0:blk, :] = v_scr[tm:tm + blk, :]
    mix = jnp.dot(a_scr[...], wo_ref[...], preferred_element_type=_F32)
    o_ref[...] = x_ref[...] + mix


def _attn_tables(sinks):
    perm = np.asarray(HEAD_PERM)
    slopes = np.exp2(-8.0 * np.arange(1, N_Q_HEADS + 1, dtype=np.float64) / N_Q_HEADS)
    qi = np.arange(WINDOW)[:, None]
    kc = np.arange(WINDOW)[None, :]
    dist = ((qi - kc) % WINDOW).astype(np.float64)
    bias = -slopes[perm][:, None, None] * dist[None]
    bias = jnp.asarray(bias.reshape(2, 2 * GROUP * WINDOW, WINDOW), _F32)
    sink = sinks.astype(_F32)[perm]
    sink = jnp.broadcast_to(sink[:, None, None], (N_Q_HEADS, WINDOW, LANES))
    return bias, sink.reshape(2, 2 * GROUP * WINDOW, LANES)


def _attn_mixer(x, gain, w_qkv, q_gain, k_gain, bias, sink, w_o, tm):
    bsz, seq, d = x.shape
    kvw = N_KV_HEADS * HEAD_DIM
    rows = 2 * GROUP * WINDOW
    kern = functools.partial(_attn_kernel, tm=tm)
    return pl.pallas_call(
        kern,
        grid=(bsz, seq // tm),
        in_specs=[
            _token_spec(tm),
            _const_spec((1, d)),
            _const_spec((d, d + 2 * kvw)),
            _const_spec((1, LANES)),
            _const_spec((1, LANES)),
            _const_spec((2, rows, WINDOW)),
            _const_spec((2, rows, LANES)),
            _const_spec((d, d)),
        ],
        out_specs=_token_spec(tm),
        out_shape=jax.ShapeDtypeStruct(x.shape, _F32),
        scratch_shapes=[
            pltpu.VMEM((tm, d), _BF16),
            pltpu.VMEM((tm + WINDOW, kvw), _BF16),
            pltpu.VMEM((tm + WINDOW, kvw), _BF16),
            pltpu.VMEM((tm, d), _BF16),
        ],
        compiler_params=_params(),
        name="swa_attention",
    )(x, gain, w_qkv, q_gain, k_gain, bias, sink, w_o)


def _chunk_interleave(w, parts, chunk):
    d, total = w.shape
    n = total // parts
    return (w.reshape(d, parts, n // chunk, chunk).transpose(0, 2, 1, 3)
            .reshape(d, total))


def kernel(x, conv_w_in, conv_w, conv_w_out, attn_w_qkv, attn_q_gain, attn_k_gain,
           attn_sinks, attn_w_o, norm_mixer, norm_ffn, ffn_w_gate_up, ffn_w_down):
    tm = TOKEN_TILE
    d = D_MODEL
    perm = np.asarray(HEAD_PERM)

    w_in = _chunk_interleave(conv_w_in[0], 3, CONV_CHUNK).astype(_BF16)
    x = _conv_mixer(x, norm_mixer[0][None, :], w_in, conv_w[0],
                    conv_w_out[0].astype(_BF16), tm)
    w_gu = _chunk_interleave(ffn_w_gate_up[0], 2, FFN_CHUNK).astype(_BF16)
    x = _ffn(x, norm_ffn[0][None, :], w_gu, ffn_w_down[0].astype(_BF16), tm)

    w_qkv = attn_w_qkv[0]
    w_q = w_qkv[:, :d].reshape(d, N_Q_HEADS, HEAD_DIM)[:, perm].reshape(d, d)
    w_qkv = jnp.concatenate([w_q, w_qkv[:, d:]], axis=1).astype(_BF16)
    w_o = attn_w_o[0].reshape(N_Q_HEADS, HEAD_DIM, d)[perm].reshape(d, d).astype(_BF16)
    q_gain = jnp.tile(attn_q_gain[0], 2)[None, :]
    k_gain = jnp.tile(attn_k_gain[0], 2)[None, :]
    bias, sink = _attn_tables(attn_sinks[0])
    x = _attn_mixer(x, norm_mixer[1][None, :], w_qkv, q_gain, k_gain, bias, sink, w_o, tm)
    w_gu = _chunk_interleave(ffn_w_gate_up[1], 2, FFN_CHUNK).astype(_BF16)
    x = _ffn(x, norm_ffn[1][None, :], w_gu, ffn_w_down[1].astype(_BF16), tm)
    return x
```

```python
import functools

import numpy as np
import jax
import jax.numpy as jnp
from jax import lax
from jax.experimental import pallas as pl
from jax.experimental.pallas import tpu as pltpu

D_MODEL = 1024
HEAD_DIM = 64
N_Q_HEADS = 16
N_KV_HEADS = 4
GROUP = N_Q_HEADS // N_KV_HEADS
WINDOW = 128
CONV_WIDTH = 3
D_FF = 2816
EPS = 1e-6

LANES = 128
SUBLANES = 8
TOKEN_TILE = 1024
CONV_CHUNK = 256
FFN_CHUNK = 256
VMEM_LIMIT_BYTES = 56 * 1024 * 1024
MASK_VALUE = -1e30

HEAD_PERM = (0, 4, 1, 5, 2, 6, 3, 7, 8, 12, 9, 13, 10, 14, 11, 15)

_F32 = jnp.float32
_BF16 = jnp.bfloat16


def _rmsnorm_bf16(x, gain):
    ms = jnp.mean(x * x, axis=-1, keepdims=True)
    return ((x * lax.rsqrt(ms + EPS)) * gain).astype(_BF16)


def _const_spec(shape):
    return pl.BlockSpec(shape, lambda b, s: (0,) * len(shape),
                        pipeline_mode=pl.Buffered(1))


def _token_spec(tm):
    return pl.BlockSpec((None, tm, D_MODEL), lambda b, s: (b, s, 0))


def _params():
    return pltpu.CompilerParams(
        dimension_semantics=("arbitrary", "arbitrary"),
        vmem_limit_bytes=VMEM_LIMIT_BYTES)


def _conv_kernel(x_ref, g_ref, win_ref, cw_ref, wout_ref, o_ref,
                 h_scr, u_scr, z_scr, *, tm):
    cw = CONV_CHUNK
    @pl.when(pl.program_id(1) == 0)
    def _():
        u_scr[0:SUBLANES, :] = jnp.zeros((SUBLANES, D_MODEL), _F32)

    h_scr[...] = _rmsnorm_bf16(x_ref[...], g_ref[...])
    for j in range(D_MODEL // cw):
        cols = slice(j * cw, (j + 1) * cw)
        b_gate, c_gate, xv = [
            jnp.dot(h_scr[...], win_ref[:, part * D_MODEL + j * cw:part * D_MODEL + (j + 1) * cw],
                    preferred_element_type=_F32)
            for part in range(3)]
        u = b_gate * xv
        u_scr[SUBLANES:SUBLANES + tm, cols] = u
        w = cw_ref[:, cols]
        y = (w[2:3] * u
             + w[1:2] * u_scr[SUBLANES - 1:SUBLANES - 1 + tm, cols]
             + w[0:1] * u_scr[SUBLANES - 2:SUBLANES - 2 + tm, cols])
        z_scr[:, cols] = (c_gate * y).astype(_BF16)
        u_scr[0:SUBLANES, cols] = u_scr[tm:tm + SUBLANES, cols]
    mix = jnp.dot(z_scr[...], wout_ref[...], preferred_element_type=_F32)
    o_ref[...] = x_ref[...] + mix


def _conv_mixer(x, gain, w_in, conv_w, w_out, tm):
    bsz, seq, d = x.shape
    kern = functools.partial(_conv_kernel, tm=tm)
    return pl.pallas_call(
        kern,
        grid=(bsz, seq // tm),
        in_specs=[
            _token_spec(tm),
            _const_spec((1, d)),
            _const_spec((d, 3 * d)),
            _const_spec((CONV_WIDTH, d)),
            _const_spec((d, d)),
        ],
        out_specs=_token_spec(tm),
        out_shape=jax.ShapeDtypeStruct(x.shape, _F32),
        scratch_shapes=[
            pltpu.VMEM((tm, d), _BF16),
            pltpu.VMEM((tm + SUBLANES, d), _F32),
            pltpu.VMEM((tm, d), _BF16),
        ],
        compiler_params=_params(),
        name="conv_mixer",
    )(x, gain, w_in, conv_w, w_out)


def _ffn_kernel(x_ref, g_ref, wgu_ref, wd_ref, o_ref, h_scr, a_scr):
    ck = FFN_CHUNK
    h_scr[...] = _rmsnorm_bf16(x_ref[...], g_ref[...])
    for c in range(D_FF // ck):
        g = jnp.dot(h_scr[...], wgu_ref[:, c * ck:(c + 1) * ck],
                    preferred_element_type=_F32)
        u = jnp.dot(h_scr[...], wgu_ref[:, D_FF + c * ck:D_FF + (c + 1) * ck],
                    preferred_element_type=_F32)
        a_scr[:, c * ck:(c + 1) * ck] = (g * jax.nn.sigmoid(g) * u).astype(_BF16)
    o_ref[...] = x_ref[...] + jnp.dot(a_scr[...], wd_ref[...],
                                      preferred_element_type=_F32)


def _ffn(x, gain, w_gu, w_down, tm):
    bsz, seq, d = x.shape
    return pl.pallas_call(
        _ffn_kernel,
        grid=(bsz, seq // tm),
        in_specs=[
            _token_spec(tm),
            _const_spec((1, d)),
            _const_spec((d, 2 * D_FF)),
            _const_spec((D_FF, d)),
        ],
        out_specs=_token_spec(tm),
        out_shape=jax.ShapeDtypeStruct(x.shape, _F32),
        scratch_shapes=[
            pltpu.VMEM((tm, d), _BF16),
            pltpu.VMEM((tm, D_FF), _BF16),
        ],
        compiler_params=_params(),
        name="swiglu_ffn",
    )(x, gain, w_gu, w_down)


def _pair_rmsnorm(t, gain, scale):
    lane = lax.broadcasted_iota(jnp.int32, t.shape, 1)
    lo = lane < HEAD_DIM
    sq = t * t
    s_lo = jnp.sum(jnp.where(lo, sq, 0.0), axis=-1, keepdims=True)
    s_hi = jnp.sum(jnp.where(lo, 0.0, sq), axis=-1, keepdims=True)
    r = jnp.where(lo, lax.rsqrt(s_lo / HEAD_DIM + EPS), lax.rsqrt(s_hi / HEAD_DIM + EPS))
    out = (t * r) * gain
    if scale != 1.0:
        out = out * scale
    return out.astype(_BF16)


def _attn_kernel(x_ref, g_ref, wqkv_ref, qg_ref, kg_ref, bias_ref, sink_ref, wo_ref,
                 o_ref, q_scr, k_scr, v_scr, a_scr, *, tm):
    blk = WINDOW
    d = D_MODEL
    kvw = N_KV_HEADS * HEAD_DIM
    seq_tile = pl.program_id(1)

    @pl.when(seq_tile == 0)
    def _():
        k_scr[0:blk, :] = jnp.zeros((blk, kvw), _BF16)
        v_scr[0:blk, :] = jnp.zeros((blk, kvw), _BF16)

    h = _rmsnorm_bf16(x_ref[...], g_ref[...])
    for half in range(2):
        q = jnp.dot(h, wqkv_ref[:, half * 512:(half + 1) * 512],
                    preferred_element_type=_F32)
        for j in range(4):
            pb = half * 4 + j
            q_scr[:, pb * LANES:(pb + 1) * LANES] = _pair_rmsnorm(
                q[:, j * LANES:(j + 1) * LANES], qg_ref[...], HEAD_DIM ** -0.5)
    kv = jnp.dot(h, wqkv_ref[:, d:d + 2 * kvw], preferred_element_type=_F32)
    for p in range(2):
        k_scr[blk:blk + tm, p * LANES:(p + 1) * LANES] = _pair_rmsnorm(
            kv[:, p * LANES:(p + 1) * LANES], kg_ref[...], 1.0)
    v_scr[blk:blk + tm, :] = kv[:, kvw:].astype(_BF16)

    lo_lane = lax.broadcasted_iota(jnp.int32, (blk, LANES), 1) < HEAD_DIM
    lo_row = lax.broadcasted_iota(jnp.int32, (LANES, blk), 0) < HEAD_DIM
    cols = 2 * GROUP * blk
    kc = lax.broadcasted_iota(jnp.int32, (blk, cols), 0)
    qi = lax.broadcasted_iota(jnp.int32, (blk, cols), 1) % blk
    in_cur = kc <= qi
    first_pen = jnp.where(seq_tile == 0, MASK_VALUE, 0.0)

    for qb in range(tm // blk):
        r0 = qb * blk
        for p in range(2):
            parts = []
            for j in range(GROUP):
                qp = q_scr[r0:r0 + blk, (p * GROUP + j) * LANES:(p * GROUP + j + 1) * LANES]
                parts.append(jnp.where(lo_lane, qp, jnp.zeros_like(qp)))
                parts.append(jnp.where(lo_lane, jnp.zeros_like(qp), qp))
            qm = jnp.concatenate(parts, axis=0)
            kw = k_scr[r0:r0 + 2 * blk, p * LANES:(p + 1) * LANES]
            s2 = lax.dot_general(kw, qm, (((1,), (1,)), ((), ())),
                                 preferred_element_type=_F32)
            s_prev = s2[:blk] + first_pen if qb == 0 else s2[:blk]
            s = jnp.where(in_cur, s2[blk:], s_prev) + bias_ref[p]
            sink = sink_ref[p]
            m = jnp.maximum(jnp.max(s, axis=0, keepdims=True), sink)
            e = jnp.exp(s - m)
            denom = jnp.sum(e, axis=0, keepdims=True) + jnp.exp(sink - m)
            pr = e * (1.0 / denom)
            zero = jnp.zeros_like(pr)
            p2 = jnp.concatenate([jnp.where(in_cur, zero, pr),
                                  jnp.where(in_cur, pr, zero)], axis=0).astype(_BF16)
            vw = v_scr[r0:r0 + 2 * blk, p * LANES:(p + 1) * LANES]
            o = lax.dot_general(vw, p2, (((0,), (0,)), ((), ())),
                                preferred_element_type=_F32)
            for j in range(GROUP):
                oj = jnp.where(lo_row, o[:, (2 * j) * blk:(2 * j + 1) * blk],
                               o[:, (2 * j + 1) * blk:(2 * j + 2) * blk])
                a_scr[r0:r0 + blk, (p * GROUP + j) * LANES:(p * GROUP + j + 1) * LANES] = (
                    oj.T.astype(_BF16))

    k_scr[0:blk, :] = k_scr[tm:tm + blk, :]
    v_scr[0:blk, :] = v_scr[tm:tm + blk, :]
    mix = jnp.dot(a_scr[...], wo_ref[...], preferred_element_type=_F32)
    o_ref[...] = x_ref[...] + mix


def _attn_tables(sinks):
    perm = np.asarray(HEAD_PERM)
    slopes = np.exp2(-8.0 * np.arange(1, N_Q_HEADS + 1, dtype=np.float64) / N_Q_HEADS)
    kc = np.arange(WINDOW)[:, None]
    qi = np.arange(WINDOW)[None, :]
    dist = ((qi - kc) % WINDOW).astype(np.float64)
    bias = -slopes[perm][:, None, None] * dist[None]
    bias = bias.reshape(2, 2 * GROUP, WINDOW, WINDOW).transpose(0, 2, 1, 3)
    bias = jnp.asarray(bias.reshape(2, WINDOW, 2 * GROUP * WINDOW), _F32)
    sink = jnp.repeat(sinks.astype(_F32)[perm], WINDOW)
    return bias, sink.reshape(2, 1, 2 * GROUP * WINDOW)


def _attn_mixer(x, gain, w_qkv, q_gain, k_gain, bias, sink, w_o, tm):
    bsz, seq, d = x.shape
    kvw = N_KV_HEADS * HEAD_DIM
    cols = 2 * GROUP * WINDOW
    kern = functools.partial(_attn_kernel, tm=tm)
    return pl.pallas_call(
        kern,
        grid=(bsz, seq // tm),
        in_specs=[
            _token_spec(tm),
            _const_spec((1, d)),
            _const_spec((d, d + 2 * kvw)),
            _const_spec((1, LANES)),
            _const_spec((1, LANES)),
            _const_spec((2, WINDOW, cols)),
            _const_spec((2, 1, cols)),
            _const_spec((d, d)),
        ],
        out_specs=_token_spec(tm),
        out_shape=jax.ShapeDtypeStruct(x.shape, _F32),
        scratch_shapes=[
            pltpu.VMEM((tm, d), _BF16),
            pltpu.VMEM((tm + WINDOW, kvw), _BF16),
            pltpu.VMEM((tm + WINDOW, kvw), _BF16),
            pltpu.VMEM((tm, d), _BF16),
        ],
        compiler_params=_params(),
        name="swa_attention",
    )(x, gain, w_qkv, q_gain, k_gain, bias, sink, w_o)


def kernel(x, conv_w_in, conv_w, conv_w_out, attn_w_qkv, attn_q_gain, attn_k_gain,
           attn_sinks, attn_w_o, norm_mixer, norm_ffn, ffn_w_gate_up, ffn_w_down):
    tm = TOKEN_TILE
    d = D_MODEL
    perm = np.asarray(HEAD_PERM)

    x = _conv_mixer(x, norm_mixer[0][None, :], conv_w_in[0].astype(_BF16), conv_w[0],
                    conv_w_out[0].astype(_BF16), tm)
    x = _ffn(x, norm_ffn[0][None, :], ffn_w_gate_up[0].astype(_BF16),
             ffn_w_down[0].astype(_BF16), tm)

    w_qkv = attn_w_qkv[0].astype(_BF16)
    w_q = w_qkv[:, :d].reshape(d, N_Q_HEADS, HEAD_DIM)[:, perm].reshape(d, d)
    w_qkv = jnp.concatenate([w_q, w_qkv[:, d:]], axis=1)
    w_o = attn_w_o[0].astype(_BF16).reshape(N_Q_HEADS, HEAD_DIM, d)[perm].reshape(d, d)
    q_gain = jnp.tile(attn_q_gain[0], 2)[None, :]
    k_gain = jnp.tile(attn_k_gain[0], 2)[None, :]
    bias, sink = _attn_tables(attn_sinks[0])
    x = _attn_mixer(x, norm_mixer[1][None, :], w_qkv, q_gain, k_gain, bias, sink, w_o, tm)
    x = _ffn(x, norm_ffn[1][None, :], ffn_w_gate_up[1].astype(_BF16),
             ffn_w_down[1].astype(_BF16), tm)
    return x
```

```python
import functools

import numpy as np
import jax
import jax.numpy as jnp
from jax import lax
from jax.experimental import pallas as pl
from jax.experimental.pallas import tpu as pltpu

D_MODEL = 1024
HEAD_DIM = 64
N_Q_HEADS = 16
N_KV_HEADS = 4
GROUP = N_Q_HEADS // N_KV_HEADS
WINDOW = 128
CONV_WIDTH = 3
D_FF = 2816
EPS = 1e-6

LANES = 128
SUBLANES = 8
TOKEN_TILE = 1024
CONV_CHUNK = 256
FFN_CHUNK = 256
VMEM_LIMIT_BYTES = 56 * 1024 * 1024
MASK_VALUE = -1e30
ATTN_SKEW = 2

HEAD_PERM = (0, 4, 1, 5, 2, 6, 3, 7, 8, 12, 9, 13, 10, 14, 11, 15)

_F32 = jnp.float32
_BF16 = jnp.bfloat16


def _rmsnorm_bf16(x, gain):
    ms = jnp.mean(x * x, axis=-1, keepdims=True)
    return ((x * lax.rsqrt(ms + EPS)) * gain).astype(_BF16)


def _const_spec(shape):
    return pl.BlockSpec(shape, lambda b, s: (0,) * len(shape),
                        pipeline_mode=pl.Buffered(1))


def _token_spec(tm):
    return pl.BlockSpec((None, tm, D_MODEL), lambda b, s: (b, s, 0))


def _params():
    return pltpu.CompilerParams(
        dimension_semantics=("arbitrary", "arbitrary"),
        vmem_limit_bytes=VMEM_LIMIT_BYTES)


def _conv_kernel(x_ref, g_ref, win_ref, cw_ref, wout_ref, o_ref,
                 h_scr, u_scr, z_scr, *, tm):
    cw = CONV_CHUNK
    @pl.when(pl.program_id(1) == 0)
    def _():
        u_scr[0:SUBLANES, :] = jnp.zeros((SUBLANES, D_MODEL), _F32)

    h_scr[...] = _rmsnorm_bf16(x_ref[...], g_ref[...])
    for j in range(D_MODEL // cw):
        cols = slice(j * cw, (j + 1) * cw)
        b_gate, c_gate, xv = [
            jnp.dot(h_scr[...], win_ref[:, part * D_MODEL + j * cw:part * D_MODEL + (j + 1) * cw],
                    preferred_element_type=_F32)
            for part in range(3)]
        u = b_gate * xv
        u_scr[SUBLANES:SUBLANES + tm, cols] = u
        w = cw_ref[:, cols]
        y = (w[2:3] * u
             + w[1:2] * u_scr[SUBLANES - 1:SUBLANES - 1 + tm, cols]
             + w[0:1] * u_scr[SUBLANES - 2:SUBLANES - 2 + tm, cols])
        z_scr[:, cols] = (c_gate * y).astype(_BF16)
        u_scr[0:SUBLANES, cols] = u_scr[tm:tm + SUBLANES, cols]
    mix = jnp.dot(z_scr[...], wout_ref[...], preferred_element_type=_F32)
    o_ref[...] = x_ref[...] + mix


def _conv_mixer(x, gain, w_in, conv_w, w_out, tm):
    bsz, seq, d = x.shape
    kern = functools.partial(_conv_kernel, tm=tm)
    return pl.pallas_call(
        kern,
        grid=(bsz, seq // tm),
        in_specs=[
            _token_spec(tm),
            _const_spec((1, d)),
            _const_spec((d, 3 * d)),
            _const_spec((CONV_WIDTH, d)),
            _const_spec((d, d)),
        ],
        out_specs=_token_spec(tm),
        out_shape=jax.ShapeDtypeStruct(x.shape, _F32),
        scratch_shapes=[
            pltpu.VMEM((tm, d), _BF16),
            pltpu.VMEM((tm + SUBLANES, d), _F32),
            pltpu.VMEM((tm, d), _BF16),
        ],
        compiler_params=_params(),
        name="conv_mixer",
    )(x, gain, w_in, conv_w, w_out)


def _ffn_kernel(x_ref, g_ref, wgu_ref, wd_ref, o_ref, h_scr, a_scr):
    ck = FFN_CHUNK
    h_scr[...] = _rmsnorm_bf16(x_ref[...], g_ref[...])
    for c in range(D_FF // ck):
        g = jnp.dot(h_scr[...], wgu_ref[:, c * ck:(c + 1) * ck],
                    preferred_element_type=_F32)
        u = jnp.dot(h_scr[...], wgu_ref[:, D_FF + c * ck:D_FF + (c + 1) * ck],
                    preferred_element_type=_F32)
        a_scr[:, c * ck:(c + 1) * ck] = (g * jax.nn.sigmoid(g) * u).astype(_BF16)
    o_ref[...] = x_ref[...] + jnp.dot(a_scr[...], wd_ref[...],
                                      preferred_element_type=_F32)


def _ffn(x, gain, w_gu, w_down, tm):
    bsz, seq, d = x.shape
    return pl.pallas_call(
        _ffn_kernel,
        grid=(bsz, seq // tm),
        in_specs=[
            _token_spec(tm),
            _const_spec((1, d)),
            _const_spec((d, 2 * D_FF)),
            _const_spec((D_FF, d)),
        ],
        out_specs=_token_spec(tm),
        out_shape=jax.ShapeDtypeStruct(x.shape, _F32),
        scratch_shapes=[
            pltpu.VMEM((tm, d), _BF16),
            pltpu.VMEM((tm, D_FF), _BF16),
        ],
        compiler_params=_params(),
        name="swiglu_ffn",
    )(x, gain, w_gu, w_down)


def _pair_rmsnorm(t, gain, scale):
    lane = lax.broadcasted_iota(jnp.int32, t.shape, 1)
    lo = lane < HEAD_DIM
    sq = t * t
    s_lo = jnp.sum(jnp.where(lo, sq, 0.0), axis=-1, keepdims=True)
    s_hi = jnp.sum(jnp.where(lo, 0.0, sq), axis=-1, keepdims=True)
    r = jnp.where(lo, lax.rsqrt(s_lo / HEAD_DIM + EPS), lax.rsqrt(s_hi / HEAD_DIM + EPS))
    out = (t * r) * gain
    if scale != 1.0:
        out = out * scale
    return out.astype(_BF16)


def _attn_kernel(x_ref, g_ref, wqkv_ref, qg_ref, kg_ref, bias_ref, sink_ref, wo_ref,
                 o_ref, q_scr, k_scr, v_scr, a_scr, *, tm):
    blk = WINDOW
    d = D_MODEL
    kvw = N_KV_HEADS * HEAD_DIM
    seq_tile = pl.program_id(1)

    @pl.when(seq_tile == 0)
    def _():
        k_scr[0:blk, :] = jnp.zeros((blk, kvw), _BF16)
        v_scr[0:blk, :] = jnp.zeros((blk, kvw), _BF16)

    h = _rmsnorm_bf16(x_ref[...], g_ref[...])
    for half in range(2):
        q = jnp.dot(h, wqkv_ref[:, half * 512:(half + 1) * 512],
                    preferred_element_type=_F32)
        for j in range(4):
            pb = half * 4 + j
            q_scr[:, pb * LANES:(pb + 1) * LANES] = _pair_rmsnorm(
                q[:, j * LANES:(j + 1) * LANES], qg_ref[...], HEAD_DIM ** -0.5)
    kv = jnp.dot(h, wqkv_ref[:, d:d + 2 * kvw], preferred_element_type=_F32)
    for p in range(2):
        k_scr[blk:blk + tm, p * LANES:(p + 1) * LANES] = _pair_rmsnorm(
            kv[:, p * LANES:(p + 1) * LANES], kg_ref[...], 1.0)
    v_scr[blk:blk + tm, :] = kv[:, kvw:].astype(_BF16)

    lo_lane = lax.broadcasted_iota(jnp.int32, (blk, LANES), 1) < HEAD_DIM
    lo_row = lax.broadcasted_iota(jnp.int32, (LANES, blk), 0) < HEAD_DIM
    cols = 2 * blk
    kc = lax.broadcasted_iota(jnp.int32, (blk, cols), 0)
    qi = lax.broadcasted_iota(jnp.int32, (blk, cols), 1) % blk
    in_cur = kc <= qi
    first_pen = jnp.where(seq_tile == 0, MASK_VALUE, 0.0)

    def scores(qb, pb):
        r0, p = qb * blk, pb // GROUP
        qp = q_scr[r0:r0 + blk, pb * LANES:(pb + 1) * LANES]
        zq = jnp.zeros_like(qp)
        qm = jnp.concatenate([jnp.where(lo_lane, qp, zq),
                              jnp.where(lo_lane, zq, qp)], axis=0)
        kw = k_scr[r0:r0 + 2 * blk, p * LANES:(p + 1) * LANES]
        return lax.dot_general(kw, qm, (((1,), (1,)), ((), ())),
                               preferred_element_type=_F32)

    def softmax(qb, pb, s2):
        s_prev = s2[:blk] + first_pen if qb == 0 else s2[:blk]
        s = jnp.where(in_cur, s2[blk:], s_prev) + bias_ref[pb]
        sink = sink_ref[pb]
        m = jnp.maximum(jnp.max(s, axis=0, keepdims=True), sink)
        e = jnp.exp(s - m)
        denom = jnp.sum(e, axis=0, keepdims=True) + jnp.exp(sink - m)
        pr = e * (1.0 / denom)
        zero = jnp.zeros_like(pr)
        return jnp.concatenate([jnp.where(in_cur, zero, pr),
                                jnp.where(in_cur, pr, zero)], axis=0).astype(_BF16)

    def weighted_values(qb, pb, p2):
        r0, p = qb * blk, pb // GROUP
        vw = v_scr[r0:r0 + 2 * blk, p * LANES:(p + 1) * LANES]
        o = lax.dot_general(vw, p2, (((0,), (0,)), ((), ())),
                            preferred_element_type=_F32)
        oj = jnp.where(lo_row, o[:, :blk], o[:, blk:])
        a_scr[r0:r0 + blk, pb * LANES:(pb + 1) * LANES] = oj.T.astype(_BF16)

    chains = [(qb, pb) for qb in range(tm // blk) for pb in range(N_Q_HEADS // 2)]
    s2_vals, p2_vals = {}, {}
    for step in range(len(chains) + 2 * ATTN_SKEW):
        if step < len(chains):
            s2_vals[step] = scores(*chains[step])
        i = step - ATTN_SKEW
        if 0 <= i < len(chains):
            p2_vals[i] = softmax(*chains[i], s2_vals.pop(i))
        i = step - 2 * ATTN_SKEW
        if 0 <= i < len(chains):
            weighted_values(*chains[i], p2_vals.pop(i))

    k_scr[0:blk, :] = k_scr[tm:tm + blk, :]
    v_scr[0:blk, :] = v_scr[tm:tm + blk, :]
    mix = jnp.dot(a_scr[...], wo_ref[...], preferred_element_type=_F32)
    o_ref[...] = x_ref[...] + mix


def _attn_tables(sinks):
    perm = np.asarray(HEAD_PERM)
    slopes = np.exp2(-8.0 * np.arange(1, N_Q_HEADS + 1, dtype=np.float64) / N_Q_HEADS)
    kc = np.arange(WINDOW)[:, None]
    qi = np.arange(WINDOW)[None, :]
    dist = ((qi - kc) % WINDOW).astype(np.float64)
    bias = -slopes[perm][:, None, None] * dist[None]
    bias = bias.reshape(N_Q_HEADS // 2, 2, WINDOW, WINDOW).transpose(0, 2, 1, 3)
    bias = jnp.asarray(bias.reshape(N_Q_HEADS // 2, WINDOW, 2 * WINDOW), _F32)
    sink = jnp.repeat(sinks.astype(_F32)[perm], WINDOW)
    return bias, sink.reshape(N_Q_HEADS // 2, 1, 2 * WINDOW)


def _attn_mixer(x, gain, w_qkv, q_gain, k_gain, bias, sink, w_o, tm):
    bsz, seq, d = x.shape
    kvw = N_KV_HEADS * HEAD_DIM
    pairs = N_Q_HEADS // 2
    cols = 2 * WINDOW
    kern = functools.partial(_attn_kernel, tm=tm)
    return pl.pallas_call(
        kern,
        grid=(bsz, seq // tm),
        in_specs=[
            _token_spec(tm),
            _const_spec((1, d)),
            _const_spec((d, d + 2 * kvw)),
            _const_spec((1, LANES)),
            _const_spec((1, LANES)),
            _const_spec((pairs, WINDOW, cols)),
            _const_spec((pairs, 1, cols)),
            _const_spec((d, d)),
        ],
        out_specs=_token_spec(tm),
        out_shape=jax.ShapeDtypeStruct(x.shape, _F32),
        scratch_shapes=[
            pltpu.VMEM((tm, d), _BF16),
            pltpu.VMEM((tm + WINDOW, kvw), _BF16),
            pltpu.VMEM((tm + WINDOW, kvw), _BF16),
            pltpu.VMEM((tm, d), _BF16),
        ],
        compiler_params=_params(),
        name="swa_attention",
    )(x, gain, w_qkv, q_gain, k_gain, bias, sink, w_o)


def kernel(x, conv_w_in, conv_w, conv_w_out, attn_w_qkv, attn_q_gain, attn_k_gain,
           attn_sinks, attn_w_o, norm_mixer, norm_ffn, ffn_w_gate_up, ffn_w_down):
    tm = TOKEN_TILE
    d = D_MODEL
    perm = np.asarray(HEAD_PERM)

    x = _conv_mixer(x, norm_mixer[0][None, :], conv_w_in[0].astype(_BF16), conv_w[0],
                    conv_w_out[0].astype(_BF16), tm)
    x = _ffn(x, norm_ffn[0][None, :], ffn_w_gate_up[0].astype(_BF16),
             ffn_w_down[0].astype(_BF16), tm)

    w_qkv = attn_w_qkv[0].astype(_BF16)
    w_q = w_qkv[:, :d].reshape(d, N_Q_HEADS, HEAD_DIM)[:, perm].reshape(d, d)
    w_qkv = jnp.concatenate([w_q, w_qkv[:, d:]], axis=1)
    w_o = attn_w_o[0].astype(_BF16).reshape(N_Q_HEADS, HEAD_DIM, d)[perm].reshape(d, d)
    q_gain = jnp.tile(attn_q_gain[0], 2)[None, :]
    k_gain = jnp.tile(attn_k_gain[0], 2)[None, :]
    bias, sink = _attn_tables(attn_sinks[0])
    x = _attn_mixer(x, norm_mixer[1][None, :], w_qkv, q_gain, k_gain, bias, sink, w_o, tm)
    x = _ffn(x, norm_ffn[1][None, :], ffn_w_gate_up[1].astype(_BF16),
             ffn_w_down[1].astype(_BF16), tm)
    return x
```

```python
import functools

import numpy as np
import jax
import jax.numpy as jnp
from jax import lax
from jax.experimental import pallas as pl
from jax.experimental.pallas import tpu as pltpu

D_MODEL = 1024
HEAD_DIM = 64
N_Q_HEADS = 16
N_KV_HEADS = 4
GROUP = N_Q_HEADS // N_KV_HEADS
WINDOW = 128
CONV_WIDTH = 3
D_FF = 2816
EPS = 1e-6

LANES = 128
SUBLANES = 8
TOKEN_TILE = 1024
CONV_CHUNK = 256
FFN_CHUNK = 256
VMEM_LIMIT_BYTES = 56 * 1024 * 1024
MASK_VALUE = -1e30
ATTN_SKEW = 2
LOG2E = 1.4426950408889634
ATTN_PARTS = 4

HEAD_PERM = (0, 4, 1, 5, 2, 6, 3, 7, 8, 12, 9, 13, 10, 14, 11, 15)

_F32 = jnp.float32
_BF16 = jnp.bfloat16


def _rmsnorm_bf16(x, gain):
    ms = jnp.mean(x * x, axis=-1, keepdims=True)
    return ((x * lax.rsqrt(ms + EPS)) * gain).astype(_BF16)


def _const_spec(shape):
    return pl.BlockSpec(shape, lambda b, s: (0,) * len(shape),
                        pipeline_mode=pl.Buffered(1))


def _token_spec(tm):
    return pl.BlockSpec((None, tm, D_MODEL), lambda b, s: (b, s, 0))


def _params():
    return pltpu.CompilerParams(
        dimension_semantics=("arbitrary", "arbitrary"),
        vmem_limit_bytes=VMEM_LIMIT_BYTES)


def _conv_kernel(x_ref, g_ref, win_ref, cw_ref, wout_ref, o_ref,
                 h_scr, u_scr, z_scr, *, tm):
    cw = CONV_CHUNK
    @pl.when(pl.program_id(1) == 0)
    def _():
        u_scr[0:SUBLANES, :] = jnp.zeros((SUBLANES, D_MODEL), _F32)

    h_scr[...] = _rmsnorm_bf16(x_ref[...], g_ref[...])
    for j in range(D_MODEL // cw):
        cols = slice(j * cw, (j + 1) * cw)
        b_gate, c_gate, xv = [
            jnp.dot(h_scr[...], win_ref[:, part * D_MODEL + j * cw:part * D_MODEL + (j + 1) * cw],
                    preferred_element_type=_F32)
            for part in range(3)]
        u = b_gate * xv
        u_scr[SUBLANES:SUBLANES + tm, cols] = u
        w = cw_ref[:, cols]
        y = (w[2:3] * u
             + w[1:2] * u_scr[SUBLANES - 1:SUBLANES - 1 + tm, cols]
             + w[0:1] * u_scr[SUBLANES - 2:SUBLANES - 2 + tm, cols])
        z_scr[:, cols] = (c_gate * y).astype(_BF16)
        u_scr[0:SUBLANES, cols] = u_scr[tm:tm + SUBLANES, cols]
    mix = jnp.dot(z_scr[...], wout_ref[...], preferred_element_type=_F32)
    o_ref[...] = x_ref[...] + mix


def _conv_mixer(x, gain, w_in, conv_w, w_out, tm):
    bsz, seq, d = x.shape
    kern = functools.partial(_conv_kernel, tm=tm)
    return pl.pallas_call(
        kern,
        grid=(bsz, seq // tm),
        in_specs=[
            _token_spec(tm),
            _const_spec((1, d)),
            _const_spec((d, 3 * d)),
            _const_spec((CONV_WIDTH, d)),
            _const_spec((d, d)),
        ],
        out_specs=_token_spec(tm),
        out_shape=jax.ShapeDtypeStruct(x.shape, _F32),
        scratch_shapes=[
            pltpu.VMEM((tm, d), _BF16),
            pltpu.VMEM((tm + SUBLANES, d), _F32),
            pltpu.VMEM((tm, d), _BF16),
        ],
        compiler_params=_params(),
        name="conv_mixer",
    )(x, gain, w_in, conv_w, w_out)


def _ffn_kernel(x_ref, g_ref, wgu_ref, wd_ref, o_ref, h_scr, a_scr):
    ck = FFN_CHUNK
    h_scr[...] = _rmsnorm_bf16(x_ref[...], g_ref[...])
    for c in range(D_FF // ck):
        g = jnp.dot(h_scr[...], wgu_ref[:, c * ck:(c + 1) * ck],
                    preferred_element_type=_F32)
        u = jnp.dot(h_scr[...], wgu_ref[:, D_FF + c * ck:D_FF + (c + 1) * ck],
                    preferred_element_type=_F32)
        a_scr[:, c * ck:(c + 1) * ck] = (g * jax.nn.sigmoid(g) * u).astype(_BF16)
    o_ref[...] = x_ref[...] + jnp.dot(a_scr[...], wd_ref[...],
                                      preferred_element_type=_F32)


def _ffn(x, gain, w_gu, w_down, tm):
    bsz, seq, d = x.shape
    return pl.pallas_call(
        _ffn_kernel,
        grid=(bsz, seq // tm),
        in_specs=[
            _token_spec(tm),
            _const_spec((1, d)),
            _const_spec((d, 2 * D_FF)),
            _const_spec((D_FF, d)),
        ],
        out_specs=_token_spec(tm),
        out_shape=jax.ShapeDtypeStruct(x.shape, _F32),
        scratch_shapes=[
            pltpu.VMEM((tm, d), _BF16),
            pltpu.VMEM((tm, D_FF), _BF16),
        ],
        compiler_params=_params(),
        name="swiglu_ffn",
    )(x, gain, w_gu, w_down)


def _pair_rmsnorm(t, gain):
    lo = lax.broadcasted_iota(jnp.int32, t.shape, 1) < HEAD_DIM
    sq = t * t
    s_lo = jnp.sum(jnp.where(lo, sq, 0.0), axis=-1, keepdims=True)
    s_hi = jnp.sum(jnp.where(lo, 0.0, sq), axis=-1, keepdims=True)
    ms = jnp.where(lo, s_lo, s_hi) * (1.0 / HEAD_DIM)
    return ((t * lax.rsqrt(ms + EPS)) * gain).astype(_BF16)


def _attn_kernel(x_ref, g_ref, wqkv_ref, qg_ref, kg_ref, bias_ref, sink_ref, wo_ref,
                 o_ref, h_scr, q_scr, k_scr, v_scr, a_scr, *, tm):
    blk = WINDOW
    d = D_MODEL
    kvw = N_KV_HEADS * HEAD_DIM
    part_rows = tm // ATTN_PARTS
    ncol = 256
    seq_tile = pl.program_id(1)

    @pl.when(seq_tile == 0)
    def _():
        k_scr[0:blk, :] = jnp.zeros((blk, kvw), _BF16)
        v_scr[0:blk, :] = jnp.zeros((blk, kvw), _BF16)

    def project_pieces(part):
        r0 = part * part_rows
        rows = slice(r0, r0 + part_rows)

        def q_piece(c):
            if c == 0:
                h_scr[rows, :] = _rmsnorm_bf16(x_ref[rows, :], g_ref[...])
            q = jnp.dot(h_scr[rows, :], wqkv_ref[:, c * ncol:(c + 1) * ncol],
                        preferred_element_type=_F32)
            for j in range(ncol // LANES):
                lanes = slice(c * ncol + j * LANES, c * ncol + (j + 1) * LANES)
                q_scr[rows, lanes] = _pair_rmsnorm(q[:, j * LANES:(j + 1) * LANES],
                                                   qg_ref[...])

        def kv_piece():
            kv = jnp.dot(h_scr[rows, :], wqkv_ref[:, d:d + 2 * kvw],
                         preferred_element_type=_F32)
            for p in range(kvw // LANES):
                k_scr[blk + r0:blk + r0 + part_rows, p * LANES:(p + 1) * LANES] = (
                    _pair_rmsnorm(kv[:, p * LANES:(p + 1) * LANES], kg_ref[...]))
            v_scr[blk + r0:blk + r0 + part_rows, :] = kv[:, kvw:].astype(_BF16)

        return [functools.partial(q_piece, c) for c in range(d // ncol)] + [kv_piece]

    def output_pieces(part):
        rows = slice(part * part_rows, (part + 1) * part_rows)

        def o_piece(c):
            lanes = slice(c * ncol, (c + 1) * ncol)
            o_ref[rows, lanes] = x_ref[rows, lanes] + jnp.dot(
                a_scr[rows, :], wo_ref[:, lanes], preferred_element_type=_F32)

        return [functools.partial(o_piece, c) for c in range(d // ncol)]

    for piece in project_pieces(0):
        piece()

    lo_lane = lax.broadcasted_iota(jnp.int32, (blk, LANES), 1) < HEAD_DIM
    lo_row = lax.broadcasted_iota(jnp.int32, (LANES, blk), 0) < HEAD_DIM
    cols = 2 * blk
    kc = lax.broadcasted_iota(jnp.int32, (blk, cols), 0)
    qi = lax.broadcasted_iota(jnp.int32, (blk, cols), 1) % blk
    in_cur = kc <= qi
    first_pen = jnp.where(seq_tile == 0, MASK_VALUE, 0.0)

    def scores(qb, pb):
        r0, p = qb * blk, pb // GROUP
        qp = q_scr[r0:r0 + blk, pb * LANES:(pb + 1) * LANES]
        zq = jnp.zeros_like(qp)
        qm = jnp.concatenate([jnp.where(lo_lane, qp, zq),
                              jnp.where(lo_lane, zq, qp)], axis=0)
        kw = k_scr[r0:r0 + 2 * blk, p * LANES:(p + 1) * LANES]
        return lax.dot_general(kw, qm, (((1,), (1,)), ((), ())),
                               preferred_element_type=_F32)

    def softmax(qb, pb, s2):
        s_prev = s2[:blk] + first_pen if qb == 0 else s2[:blk]
        s = jnp.where(in_cur, s2[blk:], s_prev) + bias_ref[pb]
        sink = sink_ref[pb]
        m = jnp.maximum(jnp.max(s, axis=0, keepdims=True), sink)
        e = jnp.exp2(s - m)
        denom = jnp.sum(e, axis=0, keepdims=True) + jnp.exp2(sink - m)
        pr = (e * (1.0 / denom)).astype(_BF16)
        zero = jnp.zeros_like(pr)
        return jnp.concatenate([jnp.where(in_cur, zero, pr),
                                jnp.where(in_cur, pr, zero)], axis=0)

    def weighted_values(qb, pb, p2):
        r0, p = qb * blk, pb // GROUP
        vw = v_scr[r0:r0 + 2 * blk, p * LANES:(p + 1) * LANES]
        o = lax.dot_general(vw, p2, (((0,), (0,)), ((), ())),
                            preferred_element_type=_F32)
        oj = jnp.where(lo_row, o[:, :blk], o[:, blk:])
        a_scr[r0:r0 + blk, pb * LANES:(pb + 1) * LANES] = oj.T.astype(_BF16)

    chains = [(qb, pb) for qb in range(tm // blk) for pb in range(N_Q_HEADS // 2)]
    part_chains = len(chains) // ATTN_PARTS
    extra = {}
    for part in range(1, ATTN_PARTS):
        pieces = project_pieces(part)
        base = (part - 1) * part_chains + 1
        for n, piece in enumerate(pieces):
            extra.setdefault(base + n * (part_chains // len(pieces)), []).append(piece)
    for part in range(ATTN_PARTS - 1):
        pieces = output_pieces(part)
        base = (part + 1) * part_chains + 2 * ATTN_SKEW + 1
        room = part_chains - 2 * ATTN_SKEW - 1
        for n, piece in enumerate(pieces):
            extra.setdefault(base + n * (room // len(pieces)), []).append(piece)

    s2_vals, p2_vals = {}, {}
    for step in range(len(chains) + 2 * ATTN_SKEW):
        if step < len(chains):
            s2_vals[step] = scores(*chains[step])
        for piece in extra.get(step, ()):
            piece()
        i = step - ATTN_SKEW
        if 0 <= i < len(chains):
            p2_vals[i] = softmax(*chains[i], s2_vals.pop(i))
        i = step - 2 * ATTN_SKEW
        if 0 <= i < len(chains):
            weighted_values(*chains[i], p2_vals.pop(i))

    for piece in output_pieces(ATTN_PARTS - 1):
        piece()
    k_scr[0:blk, :] = k_scr[tm:tm + blk, :]
    v_scr[0:blk, :] = v_scr[tm:tm + blk, :]


def _attn_tables(sinks):
    perm = np.asarray(HEAD_PERM)
    slopes = np.exp2(-8.0 * np.arange(1, N_Q_HEADS + 1, dtype=np.float64) / N_Q_HEADS)
    kc = np.arange(WINDOW)[:, None]
    qi = np.arange(WINDOW)[None, :]
    dist = ((qi - kc) % WINDOW).astype(np.float64)
    bias = -slopes[perm][:, None, None] * dist[None] * LOG2E
    bias = bias.reshape(N_Q_HEADS // 2, 2, WINDOW, WINDOW).transpose(0, 2, 1, 3)
    bias = jnp.asarray(bias.reshape(N_Q_HEADS // 2, WINDOW, 2 * WINDOW), _F32)
    sink = jnp.repeat(sinks.astype(_F32)[perm] * LOG2E, WINDOW)
    return bias, sink.reshape(N_Q_HEADS // 2, 1, 2 * WINDOW)


def _attn_mixer(x, gain, w_qkv, q_gain, k_gain, bias, sink, w_o, tm):
    bsz, seq, d = x.shape
    kvw = N_KV_HEADS * HEAD_DIM
    pairs = N_Q_HEADS // 2
    cols = 2 * WINDOW
    kern = functools.partial(_attn_kernel, tm=tm)
    return pl.pallas_call(
        kern,
        grid=(bsz, seq // tm),
        in_specs=[
            _token_spec(tm),
            _const_spec((1, d)),
            _const_spec((d, d + 2 * kvw)),
            _const_spec((1, LANES)),
            _const_spec((1, LANES)),
            _const_spec((pairs, WINDOW, cols)),
            _const_spec((pairs, 1, cols)),
            _const_spec((d, d)),
        ],
        out_specs=_token_spec(tm),
        out_shape=jax.ShapeDtypeStruct(x.shape, _F32),
        scratch_shapes=[
            pltpu.VMEM((tm, d), _BF16),
            pltpu.VMEM((tm, d), _BF16),
            pltpu.VMEM((tm + WINDOW, kvw), _BF16),
            pltpu.VMEM((tm + WINDOW, kvw), _BF16),
            pltpu.VMEM((tm, d), _BF16),
        ],
        compiler_params=_params(),
        name="swa_attention",
    )(x, gain, w_qkv, q_gain, k_gain, bias, sink, w_o)


def kernel(x, conv_w_in, conv_w, conv_w_out, attn_w_qkv, attn_q_gain, attn_k_gain,
           attn_sinks, attn_w_o, norm_mixer, norm_ffn, ffn_w_gate_up, ffn_w_down):
    tm = TOKEN_TILE
    d = D_MODEL
    perm = np.asarray(HEAD_PERM)

    x = _conv_mixer(x, norm_mixer[0][None, :], conv_w_in[0].astype(_BF16), conv_w[0],
                    conv_w_out[0].astype(_BF16), tm)
    x = _ffn(x, norm_ffn[0][None, :], ffn_w_gate_up[0].astype(_BF16),
             ffn_w_down[0].astype(_BF16), tm)

    w_qkv = attn_w_qkv[0].astype(_BF16)
    w_q = w_qkv[:, :d].reshape(d, N_Q_HEADS, HEAD_DIM)[:, perm].reshape(d, d)
    w_qkv = jnp.concatenate([w_q, w_qkv[:, d:]], axis=1)
    w_o = attn_w_o[0].astype(_BF16).reshape(N_Q_HEADS, HEAD_DIM, d)[perm].reshape(d, d)
    q_gain = jnp.tile(attn_q_gain[0], 2)[None, :] * (LOG2E * HEAD_DIM ** -0.5)
    k_gain = jnp.tile(attn_k_gain[0], 2)[None, :]
    bias, sink = _attn_tables(attn_sinks[0])
    x = _attn_mixer(x, norm_mixer[1][None, :], w_qkv, q_gain, k_gain, bias, sink, w_o, tm)
    x = _ffn(x, norm_ffn[1][None, :], ffn_w_gate_up[1].astype(_BF16),
             ffn_w_down[1].astype(_BF16), tm)
    return x
```

```python
import functools

import numpy as np
import jax
import jax.numpy as jnp
from jax import lax
from jax.experimental import pallas as pl
from jax.experimental.pallas import tpu as pltpu

D_MODEL = 1024
HEAD_DIM = 64
N_Q_HEADS = 16
N_KV_HEADS = 4
GROUP = N_Q_HEADS // N_KV_HEADS
WINDOW = 128
CONV_WIDTH = 3
D_FF = 2816
EPS = 1e-6

LANES = 128
SUBLANES = 8
TOKEN_TILE = 1024
CONV_CHUNK = 256
FFN_CHUNK = 256
VMEM_LIMIT_BYTES = 56 * 1024 * 1024
MASK_VALUE = -1e30
ATTN_SKEW = 2
LOG2E = 1.4426950408889634
ATTN_PARTS = 4

HEAD_PERM = (0, 4, 1, 5, 2, 6, 3, 7, 8, 12, 9, 13, 10, 14, 11, 15)

_F32 = jnp.float32
_BF16 = jnp.bfloat16


def _rmsnorm_bf16(x, gain):
    ms = jnp.mean(x * x, axis=-1, keepdims=True)
    return ((x * lax.rsqrt(ms + EPS)) * gain).astype(_BF16)


def _const_spec(shape):
    return pl.BlockSpec(shape, lambda b, s: (0,) * len(shape),
                        pipeline_mode=pl.Buffered(1))


def _token_spec(tm):
    return pl.BlockSpec((None, tm, D_MODEL), lambda b, s: (b, s, 0))


def _next_token_spec(tm, bsz, n_seq_tiles):
    def index(b, s):
        nxt = jnp.minimum(b * n_seq_tiles + s + 1, bsz * n_seq_tiles - 1)
        return (nxt // n_seq_tiles, nxt % n_seq_tiles, 0)
    return pl.BlockSpec((None, tm, D_MODEL), index)


def _is_first_step():
    return jnp.logical_and(pl.program_id(0) == 0, pl.program_id(1) == 0)


def _params():
    return pltpu.CompilerParams(
        dimension_semantics=("arbitrary", "arbitrary"),
        vmem_limit_bytes=VMEM_LIMIT_BYTES)


def _conv_kernel(x_ref, g_ref, win_ref, cw_ref, wout_ref, o_ref,
                 h_scr, u_scr, z_scr, *, tm):
    cw = CONV_CHUNK
    @pl.when(pl.program_id(1) == 0)
    def _():
        u_scr[0:SUBLANES, :] = jnp.zeros((SUBLANES, D_MODEL), _F32)

    h_scr[...] = _rmsnorm_bf16(x_ref[...], g_ref[...])
    for j in range(D_MODEL // cw):
        cols = slice(j * cw, (j + 1) * cw)
        b_gate, c_gate, xv = [
            jnp.dot(h_scr[...], win_ref[:, part * D_MODEL + j * cw:part * D_MODEL + (j + 1) * cw],
                    preferred_element_type=_F32)
            for part in range(3)]
        u = b_gate * xv
        u_scr[SUBLANES:SUBLANES + tm, cols] = u
        w = cw_ref[:, cols]
        y = (w[2:3] * u
             + w[1:2] * u_scr[SUBLANES - 1:SUBLANES - 1 + tm, cols]
             + w[0:1] * u_scr[SUBLANES - 2:SUBLANES - 2 + tm, cols])
        z_scr[:, cols] = (c_gate * y).astype(_BF16)
        u_scr[0:SUBLANES, cols] = u_scr[tm:tm + SUBLANES, cols]
    mix = jnp.dot(z_scr[...], wout_ref[...], preferred_element_type=_F32)
    o_ref[...] = x_ref[...] + mix


def _conv_mixer(x, gain, w_in, conv_w, w_out, tm):
    bsz, seq, d = x.shape
    kern = functools.partial(_conv_kernel, tm=tm)
    return pl.pallas_call(
        kern,
        grid=(bsz, seq // tm),
        in_specs=[
            _token_spec(tm),
            _const_spec((1, d)),
            _const_spec((d, 3 * d)),
            _const_spec((CONV_WIDTH, d)),
            _const_spec((d, d)),
        ],
        out_specs=_token_spec(tm),
        out_shape=jax.ShapeDtypeStruct(x.shape, _F32),
        scratch_shapes=[
            pltpu.VMEM((tm, d), _BF16),
            pltpu.VMEM((tm + SUBLANES, d), _F32),
            pltpu.VMEM((tm, d), _BF16),
        ],
        compiler_params=_params(),
        name="conv_mixer",
    )(x, gain, w_in, conv_w, w_out)


def _ffn_kernel(x_ref, g_ref, wgu_ref, wd_ref, o_ref, h_scr, a_scr):
    ck = FFN_CHUNK
    h_scr[...] = _rmsnorm_bf16(x_ref[...], g_ref[...])
    for c in range(D_FF // ck):
        g = jnp.dot(h_scr[...], wgu_ref[:, c * ck:(c + 1) * ck],
                    preferred_element_type=_F32)
        u = jnp.dot(h_scr[...], wgu_ref[:, D_FF + c * ck:D_FF + (c + 1) * ck],
                    preferred_element_type=_F32)
        a_scr[:, c * ck:(c + 1) * ck] = (g * jax.nn.sigmoid(g) * u).astype(_BF16)
    o_ref[...] = x_ref[...] + jnp.dot(a_scr[...], wd_ref[...],
                                      preferred_element_type=_F32)


def _ffn(x, gain, w_gu, w_down, tm):
    bsz, seq, d = x.shape
    return pl.pallas_call(
        _ffn_kernel,
        grid=(bsz, seq // tm),
        in_specs=[
            _token_spec(tm),
            _const_spec((1, d)),
            _const_spec((d, 2 * D_FF)),
            _const_spec((D_FF, d)),
        ],
        out_specs=_token_spec(tm),
        out_shape=jax.ShapeDtypeStruct(x.shape, _F32),
        scratch_shapes=[
            pltpu.VMEM((tm, d), _BF16),
            pltpu.VMEM((tm, D_FF), _BF16),
        ],
        compiler_params=_params(),
        name="swiglu_ffn",
    )(x, gain, w_gu, w_down)


def _pair_rmsnorm(t, gain):
    lo = lax.broadcasted_iota(jnp.int32, t.shape, 1) < HEAD_DIM
    sq = t * t
    s_lo = jnp.sum(jnp.where(lo, sq, 0.0), axis=-1, keepdims=True)
    s_hi = jnp.sum(jnp.where(lo, 0.0, sq), axis=-1, keepdims=True)
    ms = jnp.where(lo, s_lo, s_hi) * (1.0 / HEAD_DIM)
    return ((t * lax.rsqrt(ms + EPS)) * gain).astype(_BF16)


def _attn_kernel(x_ref, xn_ref, g_ref, wqkv_ref, qg_ref, kg_ref, bias_ref, sink_ref, wo_ref,
                 o_ref, h_scr, q_scr, k_scr, v_scr, a_scr, *, tm):
    blk = WINDOW
    d = D_MODEL
    kvw = N_KV_HEADS * HEAD_DIM
    part_rows = tm // ATTN_PARTS
    ncol = 256
    seq_tile = pl.program_id(1)

    @pl.when(seq_tile == 0)
    def _():
        k_scr[0:blk, :] = jnp.zeros((blk, kvw), _BF16)
        v_scr[0:blk, :] = jnp.zeros((blk, kvw), _BF16)

    def project_pieces(part, src_ref):
        r0 = part * part_rows
        rows = slice(r0, r0 + part_rows)

        def q_piece(c):
            if c == 0:
                h_scr[rows, :] = _rmsnorm_bf16(src_ref[rows, :], g_ref[...])
            q = jnp.dot(h_scr[rows, :], wqkv_ref[:, c * ncol:(c + 1) * ncol],
                        preferred_element_type=_F32)
            for j in range(ncol // LANES):
                lanes = slice(c * ncol + j * LANES, c * ncol + (j + 1) * LANES)
                q_scr[rows, lanes] = _pair_rmsnorm(q[:, j * LANES:(j + 1) * LANES],
                                                   qg_ref[...])

        def kv_piece():
            kv = jnp.dot(h_scr[rows, :], wqkv_ref[:, d:d + 2 * kvw],
                         preferred_element_type=_F32)
            for p in range(kvw // LANES):
                k_scr[blk + r0:blk + r0 + part_rows, p * LANES:(p + 1) * LANES] = (
                    _pair_rmsnorm(kv[:, p * LANES:(p + 1) * LANES], kg_ref[...]))
            v_scr[blk + r0:blk + r0 + part_rows, :] = kv[:, kvw:].astype(_BF16)

        return [functools.partial(q_piece, c) for c in range(d // ncol)] + [kv_piece]

    def output_pieces(part):
        rows = slice(part * part_rows, (part + 1) * part_rows)

        def o_piece(c):
            lanes = slice(c * ncol, (c + 1) * ncol)
            o_ref[rows, lanes] = x_ref[rows, lanes] + jnp.dot(
                a_scr[rows, :], wo_ref[:, lanes], preferred_element_type=_F32)

        return [functools.partial(o_piece, c) for c in range(d // ncol)]

    @pl.when(_is_first_step())
    def _():
        for piece in project_pieces(0, x_ref):
            piece()

    lo_lane = lax.broadcasted_iota(jnp.int32, (blk, LANES), 1) < HEAD_DIM
    lo_row = lax.broadcasted_iota(jnp.int32, (LANES, blk), 0) < HEAD_DIM
    cols = 2 * blk
    kc = lax.broadcasted_iota(jnp.int32, (blk, cols), 0)
    qi = lax.broadcasted_iota(jnp.int32, (blk, cols), 1) % blk
    in_cur = kc <= qi
    first_pen = jnp.where(seq_tile == 0, MASK_VALUE, 0.0)

    def scores(qb, pb):
        r0, p = qb * blk, pb // GROUP
        qp = q_scr[r0:r0 + blk, pb * LANES:(pb + 1) * LANES]
        zq = jnp.zeros_like(qp)
        qm = jnp.concatenate([jnp.where(lo_lane, qp, zq),
                              jnp.where(lo_lane, zq, qp)], axis=0)
        kw = k_scr[r0:r0 + 2 * blk, p * LANES:(p + 1) * LANES]
        return lax.dot_general(kw, qm, (((1,), (1,)), ((), ())),
                               preferred_element_type=_F32)

    def softmax(qb, pb, s2):
        s_prev = s2[:blk] + first_pen if qb == 0 else s2[:blk]
        s = jnp.where(in_cur, s2[blk:], s_prev) + bias_ref[pb]
        sink = sink_ref[pb]
        m = jnp.maximum(jnp.max(s, axis=0, keepdims=True), sink)
        e = jnp.exp2(s - m)
        denom = jnp.sum(e, axis=0, keepdims=True) + jnp.exp2(sink - m)
        pr = (e * (1.0 / denom)).astype(_BF16)
        zero = jnp.zeros_like(pr)
        return jnp.concatenate([jnp.where(in_cur, zero, pr),
                                jnp.where(in_cur, pr, zero)], axis=0)

    def weighted_values(qb, pb, p2):
        r0, p = qb * blk, pb // GROUP
        vw = v_scr[r0:r0 + 2 * blk, p * LANES:(p + 1) * LANES]
        o = lax.dot_general(vw, p2, (((0,), (0,)), ((), ())),
                            preferred_element_type=_F32)
        oj = jnp.where(lo_row, o[:, :blk], o[:, blk:])
        a_scr[r0:r0 + blk, pb * LANES:(pb + 1) * LANES] = oj.T.astype(_BF16)

    chains = [(qb, pb) for qb in range(tm // blk) for pb in range(N_Q_HEADS // 2)]
    part_chains = len(chains) // ATTN_PARTS
    extra = {}
    for part in range(1, ATTN_PARTS + 1):
        pieces = (project_pieces(part, x_ref) if part < ATTN_PARTS
                  else project_pieces(0, xn_ref))
        base = (part - 1) * part_chains + 1
        for n, piece in enumerate(pieces):
            extra.setdefault(base + n * (part_chains // len(pieces)), []).append(piece)
    for part in range(ATTN_PARTS - 1):
        pieces = output_pieces(part)
        base = (part + 1) * part_chains + 2 * ATTN_SKEW + 1
        room = part_chains - 2 * ATTN_SKEW - 1
        for n, piece in enumerate(pieces):
            extra.setdefault(base + n * (room // len(pieces)), []).append(piece)

    s2_vals, p2_vals = {}, {}
    for step in range(len(chains) + 2 * ATTN_SKEW):
        if step < len(chains):
            s2_vals[step] = scores(*chains[step])
        for piece in extra.get(step, ()):
            piece()
        i = step - ATTN_SKEW
        if 0 <= i < len(chains):
            p2_vals[i] = softmax(*chains[i], s2_vals.pop(i))
        i = step - 2 * ATTN_SKEW
        if 0 <= i < len(chains):
            weighted_values(*chains[i], p2_vals.pop(i))

    for piece in output_pieces(ATTN_PARTS - 1):
        piece()
    k_scr[0:blk, :] = k_scr[tm:tm + blk, :]
    v_scr[0:blk, :] = v_scr[tm:tm + blk, :]


def _attn_tables(sinks):
    perm = np.asarray(HEAD_PERM)
    slopes = np.exp2(-8.0 * np.arange(1, N_Q_HEADS + 1, dtype=np.float64) / N_Q_HEADS)
    kc = np.arange(WINDOW)[:, None]
    qi = np.arange(WINDOW)[None, :]
    dist = ((qi - kc) % WINDOW).astype(np.float64)
    bias = -slopes[perm][:, None, None] * dist[None] * LOG2E
    bias = bias.reshape(N_Q_HEADS // 2, 2, WINDOW, WINDOW).transpose(0, 2, 1, 3)
    bias = jnp.asarray(bias.reshape(N_Q_HEADS // 2, WINDOW, 2 * WINDOW), _F32)
    sink = jnp.repeat(sinks.astype(_F32)[perm] * LOG2E, WINDOW)
    return bias, sink.reshape(N_Q_HEADS // 2, 1, 2 * WINDOW)


def _attn_mixer(x, gain, w_qkv, q_gain, k_gain, bias, sink, w_o, tm):
    bsz, seq, d = x.shape
    kvw = N_KV_HEADS * HEAD_DIM
    pairs = N_Q_HEADS // 2
    cols = 2 * WINDOW
    kern = functools.partial(_attn_kernel, tm=tm)
    return pl.pallas_call(
        kern,
        grid=(bsz, seq // tm),
        in_specs=[
            _token_spec(tm),
            _next_token_spec(tm, bsz, seq // tm),
            _const_spec((1, d)),
            _const_spec((d, d + 2 * kvw)),
            _const_spec((1, LANES)),
            _const_spec((1, LANES)),
            _const_spec((pairs, WINDOW, cols)),
            _const_spec((pairs, 1, cols)),
            _const_spec((d, d)),
        ],
        out_specs=_token_spec(tm),
        out_shape=jax.ShapeDtypeStruct(x.shape, _F32),
        scratch_shapes=[
            pltpu.VMEM((tm, d), _BF16),
            pltpu.VMEM((tm, d), _BF16),
            pltpu.VMEM((tm + WINDOW, kvw), _BF16),
            pltpu.VMEM((tm + WINDOW, kvw), _BF16),
            pltpu.VMEM((tm, d), _BF16),
        ],
        compiler_params=_params(),
        name="swa_attention",
    )(x, x, gain, w_qkv, q_gain, k_gain, bias, sink, w_o)


def kernel(x, conv_w_in, conv_w, conv_w_out, attn_w_qkv, attn_q_gain, attn_k_gain,
           attn_sinks, attn_w_o, norm_mixer, norm_ffn, ffn_w_gate_up, ffn_w_down):
    tm = TOKEN_TILE
    d = D_MODEL
    perm = np.asarray(HEAD_PERM)

    x = _conv_mixer(x, norm_mixer[0][None, :], conv_w_in[0].astype(_BF16), conv_w[0],
                    conv_w_out[0].astype(_BF16), tm)
    x = _ffn(x, norm_ffn[0][None, :], ffn_w_gate_up[0].astype(_BF16),
             ffn_w_down[0].astype(_BF16), tm)

    w_qkv = attn_w_qkv[0].astype(_BF16)
    w_q = w_qkv[:, :d].reshape(d, N_Q_HEADS, HEAD_DIM)[:, perm].reshape(d, d)
    w_qkv = jnp.concatenate([w_q, w_qkv[:, d:]], axis=1)
    w_o = attn_w_o[0].astype(_BF16).reshape(N_Q_HEADS, HEAD_DIM, d)[perm].reshape(d, d)
    q_gain = jnp.tile(attn_q_gain[0], 2)[None, :] * (LOG2E * HEAD_DIM ** -0.5)
    k_gain = jnp.tile(attn_k_gain[0], 2)[None, :]
    bias, sink = _attn_tables(attn_sinks[0])
    x = _attn_mixer(x, norm_mixer[1][None, :], w_qkv, q_gain, k_gain, bias, sink, w_o, tm)
    x = _ffn(x, norm_ffn[1][None, :], ffn_w_gate_up[1].astype(_BF16),
             ffn_w_down[1].astype(_BF16), tm)
    return x
```

```python
import functools

import numpy as np
import jax
import jax.numpy as jnp
from jax import lax
from jax.experimental import pallas as pl
from jax.experimental.pallas import tpu as pltpu

D_MODEL = 1024
HEAD_DIM = 64
N_Q_HEADS = 16
N_KV_HEADS = 4
GROUP = N_Q_HEADS // N_KV_HEADS
WINDOW = 128
CONV_WIDTH = 3
D_FF = 2816
EPS = 1e-6

LANES = 128
SUBLANES = 8
BF16_SUBLANES = 16
TOKEN_TILE = 1024
CONV_CHUNK = 256
FFN_CHUNK = 256
VMEM_LIMIT_BYTES = 56 * 1024 * 1024
MASK_VALUE = -1e30
ATTN_SKEW = 2
LOG2E = 1.4426950408889634
ATTN_PARTS = 4

HEAD_PERM = (0, 4, 1, 5, 2, 6, 3, 7, 8, 12, 9, 13, 10, 14, 11, 15)

_F32 = jnp.float32
_BF16 = jnp.bfloat16


def _rmsnorm_bf16(x, gain):
    ms = jnp.mean(x * x, axis=-1, keepdims=True)
    return ((x * lax.rsqrt(ms + EPS)) * gain).astype(_BF16)


def _const_spec(shape):
    return pl.BlockSpec(shape, lambda b, s: (0,) * len(shape),
                        pipeline_mode=pl.Buffered(1))


def _token_spec(tm):
    return pl.BlockSpec((None, tm, D_MODEL), lambda b, s: (b, s, 0))


def _next_token_spec(tm, bsz, n_seq_tiles):
    def index(b, s):
        nxt = jnp.minimum(b * n_seq_tiles + s + 1, bsz * n_seq_tiles - 1)
        return (nxt // n_seq_tiles, nxt % n_seq_tiles, 0)
    return pl.BlockSpec((None, tm, D_MODEL), index)


def _is_first_step():
    return jnp.logical_and(pl.program_id(0) == 0, pl.program_id(1) == 0)


def _params():
    return pltpu.CompilerParams(
        dimension_semantics=("arbitrary", "arbitrary"),
        vmem_limit_bytes=VMEM_LIMIT_BYTES)


def _cast_rows_per_step(n_rows, n_steps):
    rows = BF16_SUBLANES
    while n_rows % rows or n_rows // rows > n_steps:
        rows += BF16_SUBLANES
    return rows


def _cast_spec(shape, n_steps, n_seq_tiles):
    layers, n_rows, n_cols = shape
    rows = _cast_rows_per_step(n_rows, n_steps)
    last = n_rows // rows - 1

    def index(b, s):
        return (0, jnp.minimum(b * n_seq_tiles + s, last), 0)
    return pl.BlockSpec((layers, rows, n_cols), index)


def _conv_kernel(x_ref, g_ref, win_ref, cw_ref, wout_ref, wgu_ref, wdn_ref,
                 o_ref, wgu_bf_ref, wdn_bf_ref, h_scr, u_scr, z_scr, *, tm):
    cw = CONV_CHUNK
    wgu_bf_ref[...] = wgu_ref[...].astype(_BF16)
    wdn_bf_ref[...] = wdn_ref[...].astype(_BF16)

    @pl.when(pl.program_id(1) == 0)
    def _():
        u_scr[0:SUBLANES, :] = jnp.zeros((SUBLANES, D_MODEL), _F32)

    h_scr[...] = _rmsnorm_bf16(x_ref[...], g_ref[...])
    for j in range(D_MODEL // cw):
        cols = slice(j * cw, (j + 1) * cw)
        b_gate, c_gate, xv = [
            jnp.dot(h_scr[...], win_ref[:, part * D_MODEL + j * cw:part * D_MODEL + (j + 1) * cw],
                    preferred_element_type=_F32)
            for part in range(3)]
        u = b_gate * xv
        u_scr[SUBLANES:SUBLANES + tm, cols] = u
        w = cw_ref[:, cols]
        y = (w[2:3] * u
             + w[1:2] * u_scr[SUBLANES - 1:SUBLANES - 1 + tm, cols]
             + w[0:1] * u_scr[SUBLANES - 2:SUBLANES - 2 + tm, cols])
        z_scr[:, cols] = (c_gate * y).astype(_BF16)
        u_scr[0:SUBLANES, cols] = u_scr[tm:tm + SUBLANES, cols]
    mix = jnp.dot(z_scr[...], wout_ref[...], preferred_element_type=_F32)
    o_ref[...] = x_ref[...] + mix


def _conv_mixer(x, gain, w_in, conv_w, w_out, ffn_w_gate_up, ffn_w_down, tm):
    bsz, seq, d = x.shape
    n_seq_tiles = seq // tm
    n_steps = bsz * n_seq_tiles
    kern = functools.partial(_conv_kernel, tm=tm)
    return pl.pallas_call(
        kern,
        grid=(bsz, n_seq_tiles),
        in_specs=[
            _token_spec(tm),
            _const_spec((1, d)),
            _const_spec((d, 3 * d)),
            _const_spec((CONV_WIDTH, d)),
            _const_spec((d, d)),
            _cast_spec(ffn_w_gate_up.shape, n_steps, n_seq_tiles),
            _cast_spec(ffn_w_down.shape, n_steps, n_seq_tiles),
        ],
        out_specs=[
            _token_spec(tm),
            _cast_spec(ffn_w_gate_up.shape, n_steps, n_seq_tiles),
            _cast_spec(ffn_w_down.shape, n_steps, n_seq_tiles),
        ],
        out_shape=[
            jax.ShapeDtypeStruct(x.shape, _F32),
            jax.ShapeDtypeStruct(ffn_w_gate_up.shape, _BF16),
            jax.ShapeDtypeStruct(ffn_w_down.shape, _BF16),
        ],
        scratch_shapes=[
            pltpu.VMEM((tm, d), _BF16),
            pltpu.VMEM((tm + SUBLANES, d), _F32),
            pltpu.VMEM((tm, d), _BF16),
        ],
        compiler_params=_params(),
        name="conv_mixer",
    )(x, gain, w_in, conv_w, w_out, ffn_w_gate_up, ffn_w_down)


def _ffn_kernel(x_ref, g_ref, wgu_ref, wd_ref, o_ref, h_scr, a_scr):
    ck = FFN_CHUNK
    h_scr[...] = _rmsnorm_bf16(x_ref[...], g_ref[...])
    for c in range(D_FF // ck):
        g = jnp.dot(h_scr[...], wgu_ref[:, c * ck:(c + 1) * ck],
                    preferred_element_type=_F32)
        u = jnp.dot(h_scr[...], wgu_ref[:, D_FF + c * ck:D_FF + (c + 1) * ck],
                    preferred_element_type=_F32)
        a_scr[:, c * ck:(c + 1) * ck] = (g * jax.nn.sigmoid(g) * u).astype(_BF16)
    o_ref[...] = x_ref[...] + jnp.dot(a_scr[...], wd_ref[...],
                                      preferred_element_type=_F32)


def _layer_spec(shape, layer):
    return pl.BlockSpec((None,) + tuple(shape[1:]), lambda b, s: (layer, 0, 0),
                        pipeline_mode=pl.Buffered(1))


def _ffn(x, gain, w_gu, w_down, layer, tm):
    bsz, seq, d = x.shape
    return pl.pallas_call(
        _ffn_kernel,
        grid=(bsz, seq // tm),
        in_specs=[
            _token_spec(tm),
            _const_spec((1, d)),
            _layer_spec(w_gu.shape, layer),
            _layer_spec(w_down.shape, layer),
        ],
        out_specs=_token_spec(tm),
        out_shape=jax.ShapeDtypeStruct(x.shape, _F32),
        scratch_shapes=[
            pltpu.VMEM((tm, d), _BF16),
            pltpu.VMEM((tm, D_FF), _BF16),
        ],
        compiler_params=_params(),
        name="swiglu_ffn",
    )(x, gain, w_gu, w_down)


def _pair_rmsnorm(t, gain):
    lo = lax.broadcasted_iota(jnp.int32, t.shape, 1) < HEAD_DIM
    sq = t * t
    s_lo = jnp.sum(jnp.where(lo, sq, 0.0), axis=-1, keepdims=True)
    s_hi = jnp.sum(jnp.where(lo, 0.0, sq), axis=-1, keepdims=True)
    ms = jnp.where(lo, s_lo, s_hi) * (1.0 / HEAD_DIM)
    return ((t * lax.rsqrt(ms + EPS)) * gain).astype(_BF16)


def _attn_kernel(x_ref, xn_ref, g_ref, wqkv_ref, qg_ref, kg_ref, bias_ref, sink_ref, wo_ref,
                 o_ref, h_scr, q_scr, k_scr, v_scr, a_scr, *, tm):
    blk = WINDOW
    d = D_MODEL
    kvw = N_KV_HEADS * HEAD_DIM
    part_rows = tm // ATTN_PARTS
    ncol = 256
    seq_tile = pl.program_id(1)

    @pl.when(seq_tile == 0)
    def _():
        k_scr[0:blk, :] = jnp.zeros((blk, kvw), _BF16)
        v_scr[0:blk, :] = jnp.zeros((blk, kvw), _BF16)

    def project_pieces(part, src_ref):
        r0 = part * part_rows
        rows = slice(r0, r0 + part_rows)

        def q_piece(c):
            if c == 0:
                h_scr[rows, :] = _rmsnorm_bf16(src_ref[rows, :], g_ref[...])
            q = jnp.dot(h_scr[rows, :], wqkv_ref[:, c * ncol:(c + 1) * ncol],
                        preferred_element_type=_F32)
            for j in range(ncol // LANES):
                lanes = slice(c * ncol + j * LANES, c * ncol + (j + 1) * LANES)
                q_scr[rows, lanes] = _pair_rmsnorm(q[:, j * LANES:(j + 1) * LANES],
                                                   qg_ref[...])

        def kv_piece():
            kv = jnp.dot(h_scr[rows, :], wqkv_ref[:, d:d + 2 * kvw],
                         preferred_element_type=_F32)
            for p in range(kvw // LANES):
                k_scr[blk + r0:blk + r0 + part_rows, p * LANES:(p + 1) * LANES] = (
                    _pair_rmsnorm(kv[:, p * LANES:(p + 1) * LANES], kg_ref[...]))
            v_scr[blk + r0:blk + r0 + part_rows, :] = kv[:, kvw:].astype(_BF16)

        return [functools.partial(q_piece, c) for c in range(d // ncol)] + [kv_piece]

    def output_pieces(part):
        rows = slice(part * part_rows, (part + 1) * part_rows)

        def o_piece(c):
            lanes = slice(c * ncol, (c + 1) * ncol)
            o_ref[rows, lanes] = x_ref[rows, lanes] + jnp.dot(
                a_scr[rows, :], wo_ref[:, lanes], preferred_element_type=_F32)

        return [functools.partial(o_piece, c) for c in range(d // ncol)]

    @pl.when(_is_first_step())
    def _():
        for piece in project_pieces(0, x_ref):
            piece()

    lo_lane = lax.broadcasted_iota(jnp.int32, (blk, LANES), 1) < HEAD_DIM
    lo_row = lax.broadcasted_iota(jnp.int32, (LANES, blk), 0) < HEAD_DIM
    cols = 2 * blk
    kc = lax.broadcasted_iota(jnp.int32, (blk, cols), 0)
    qi = lax.broadcasted_iota(jnp.int32, (blk, cols), 1) % blk
    in_cur = kc <= qi
    first_pen = jnp.where(seq_tile == 0, MASK_VALUE, 0.0)

    def scores(qb, pb):
        r0, p = qb * blk, pb // GROUP
        qp = q_scr[r0:r0 + blk, pb * LANES:(pb + 1) * LANES]
        zq = jnp.zeros_like(qp)
        qm = jnp.concatenate([jnp.where(lo_lane, qp, zq),
                              jnp.where(lo_lane, zq, qp)], axis=0)
        kw = k_scr[r0:r0 + 2 * blk, p * LANES:(p + 1) * LANES]
        return lax.dot_general(kw, qm, (((1,), (1,)), ((), ())),
                               preferred_element_type=_F32)

    def softmax(qb, pb, s2):
        s_prev = s2[:blk] + first_pen if qb == 0 else s2[:blk]
        s = jnp.where(in_cur, s2[blk:], s_prev) + bias_ref[pb]
        sink = sink_ref[pb]
        m = jnp.maximum(jnp.max(s, axis=0, keepdims=True), sink)
        e = jnp.exp2(s - m)
        denom = jnp.sum(e, axis=0, keepdims=True) + jnp.exp2(sink - m)
        pr = (e * (1.0 / denom)).astype(_BF16)
        zero = jnp.zeros_like(pr)
        return jnp.concatenate([jnp.where(in_cur, zero, pr),
                                jnp.where(in_cur, pr, zero)], axis=0)

    def weighted_values(qb, pb, p2):
        r0, p = qb * blk, pb // GROUP
        vw = v_scr[r0:r0 + 2 * blk, p * LANES:(p + 1) * LANES]
        o = lax.dot_general(vw, p2, (((0,), (0,)), ((), ())),
                            preferred_element_type=_F32)
        oj = jnp.where(lo_row, o[:, :blk], o[:, blk:])
        a_scr[r0:r0 + blk, pb * LANES:(pb + 1) * LANES] = oj.T.astype(_BF16)

    chains = [(qb, pb) for qb in range(tm // blk) for pb in range(N_Q_HEADS // 2)]
    part_chains = len(chains) // ATTN_PARTS
    extra = {}
    for part in range(1, ATTN_PARTS + 1):
        pieces = (project_pieces(part, x_ref) if part < ATTN_PARTS
                  else project_pieces(0, xn_ref))
        base = (part - 1) * part_chains + 1
        for n, piece in enumerate(pieces):
            extra.setdefault(base + n * (part_chains // len(pieces)), []).append(piece)
    for part in range(ATTN_PARTS - 1):
        pieces = output_pieces(part)
        base = (part + 1) * part_chains + 2 * ATTN_SKEW + 1
        room = part_chains - 2 * ATTN_SKEW - 1
        for n, piece in enumerate(pieces):
            extra.setdefault(base + n * (room // len(pieces)), []).append(piece)

    s2_vals, p2_vals = {}, {}
    for step in range(len(chains) + 2 * ATTN_SKEW):
        if step < len(chains):
            s2_vals[step] = scores(*chains[step])
        for piece in extra.get(step, ()):
            piece()
        i = step - ATTN_SKEW
        if 0 <= i < len(chains):
            p2_vals[i] = softmax(*chains[i], s2_vals.pop(i))
        i = step - 2 * ATTN_SKEW
        if 0 <= i < len(chains):
            weighted_values(*chains[i], p2_vals.pop(i))

    for piece in output_pieces(ATTN_PARTS - 1):
        piece()
    k_scr[0:blk, :] = k_scr[tm:tm + blk, :]
    v_scr[0:blk, :] = v_scr[tm:tm + blk, :]


def _attn_tables(sinks):
    perm = np.asarray(HEAD_PERM)
    slopes = np.exp2(-8.0 * np.arange(1, N_Q_HEADS + 1, dtype=np.float64) / N_Q_HEADS)
    kc = np.arange(WINDOW)[:, None]
    qi = np.arange(WINDOW)[None, :]
    dist = ((qi - kc) % WINDOW).astype(np.float64)
    bias = -slopes[perm][:, None, None] * dist[None] * LOG2E
    bias = bias.reshape(N_Q_HEADS // 2, 2, WINDOW, WINDOW).transpose(0, 2, 1, 3)
    bias = jnp.asarray(bias.reshape(N_Q_HEADS // 2, WINDOW, 2 * WINDOW), _F32)
    sink = jnp.repeat(sinks.astype(_F32)[perm] * LOG2E, WINDOW)
    return bias, sink.reshape(N_Q_HEADS // 2, 1, 2 * WINDOW)


def _attn_mixer(x, gain, w_qkv, q_gain, k_gain, bias, sink, w_o, tm):
    bsz, seq, d = x.shape
    kvw = N_KV_HEADS * HEAD_DIM
    pairs = N_Q_HEADS // 2
    cols = 2 * WINDOW
    kern = functools.partial(_attn_kernel, tm=tm)
    return pl.pallas_call(
        kern,
        grid=(bsz, seq // tm),
        in_specs=[
            _token_spec(tm),
            _next_token_spec(tm, bsz, seq // tm),
            _const_spec((1, d)),
            _const_spec((d, d + 2 * kvw)),
            _const_spec((1, LANES)),
            _const_spec((1, LANES)),
            _const_spec((pairs, WINDOW, cols)),
            _const_spec((pairs, 1, cols)),
            _const_spec((d, d)),
        ],
        out_specs=_token_spec(tm),
        out_shape=jax.ShapeDtypeStruct(x.shape, _F32),
        scratch_shapes=[
            pltpu.VMEM((tm, d), _BF16),
            pltpu.VMEM((tm, d), _BF16),
            pltpu.VMEM((tm + WINDOW, kvw), _BF16),
            pltpu.VMEM((tm + WINDOW, kvw), _BF16),
            pltpu.VMEM((tm, d), _BF16),
        ],
        compiler_params=_params(),
        name="swa_attention",
    )(x, x, gain, w_qkv, q_gain, k_gain, bias, sink, w_o)


def kernel(x, conv_w_in, conv_w, conv_w_out, attn_w_qkv, attn_q_gain, attn_k_gain,
           attn_sinks, attn_w_o, norm_mixer, norm_ffn, ffn_w_gate_up, ffn_w_down):
    tm = TOKEN_TILE
    d = D_MODEL
    perm = np.asarray(HEAD_PERM)

    x, w_gu, w_down = _conv_mixer(
        x, norm_mixer[0][None, :], conv_w_in[0].astype(_BF16), conv_w[0],
        conv_w_out[0].astype(_BF16), ffn_w_gate_up, ffn_w_down, tm)
    x = _ffn(x, norm_ffn[0][None, :], w_gu, w_down, 0, tm)

    w_qkv = attn_w_qkv[0].astype(_BF16)
    w_q = w_qkv[:, :d].reshape(d, N_Q_HEADS, HEAD_DIM)[:, perm].reshape(d, d)
    w_qkv = jnp.concatenate([w_q, w_qkv[:, d:]], axis=1)
    w_o = attn_w_o[0].astype(_BF16).reshape(N_Q_HEADS, HEAD_DIM, d)[perm].reshape(d, d)
    q_gain = jnp.tile(attn_q_gain[0], 2)[None, :] * (LOG2E * HEAD_DIM ** -0.5)
    k_gain = jnp.tile(attn_k_gain[0], 2)[None, :]
    bias, sink = _attn_tables(attn_sinks[0])
    x = _attn_mixer(x, norm_mixer[1][None, :], w_qkv, q_gain, k_gain, bias, sink, w_o, tm)
    x = _ffn(x, norm_ffn[1][None, :], w_gu, w_down, 1, tm)
    return x
```

```python
import functools

import numpy as np
import jax
import jax.numpy as jnp
from jax import lax
from jax.experimental import pallas as pl
from jax.experimental.pallas import tpu as pltpu

D_MODEL = 1024
HEAD_DIM = 64
N_Q_HEADS = 16
N_KV_HEADS = 4
GROUP = N_Q_HEADS // N_KV_HEADS
WINDOW = 128
CONV_WIDTH = 3
D_FF = 2816
EPS = 1e-6

LANES = 128
SUBLANES = 8
BF16_SUBLANES = 16
TOKEN_TILE = 1024
CONV_CHUNK = 256
FFN_CHUNK = 256
VMEM_LIMIT_BYTES = 56 * 1024 * 1024
MASK_VALUE = -1e30
ATTN_SKEW = 2
LOG2E = 1.4426950408889634
ATTN_PARTS = 4

HEAD_PERM = (0, 4, 1, 5, 2, 6, 3, 7, 8, 12, 9, 13, 10, 14, 11, 15)

_F32 = jnp.float32
_BF16 = jnp.bfloat16


def _rmsnorm_bf16(x, gain):
    ms = jnp.mean(x * x, axis=-1, keepdims=True)
    return ((x * lax.rsqrt(ms + EPS)) * gain).astype(_BF16)


def _const_spec(shape):
    return pl.BlockSpec(shape, lambda b, s: (0,) * len(shape),
                        pipeline_mode=pl.Buffered(1))


def _token_spec(tm):
    return pl.BlockSpec((None, tm, D_MODEL), lambda b, s: (b, s, 0))


def _next_token_spec(tm, bsz, n_seq_tiles):
    def index(b, s):
        nxt = jnp.minimum(b * n_seq_tiles + s + 1, bsz * n_seq_tiles - 1)
        return (nxt // n_seq_tiles, nxt % n_seq_tiles, 0)
    return pl.BlockSpec((None, tm, D_MODEL), index)


def _is_first_step():
    return jnp.logical_and(pl.program_id(0) == 0, pl.program_id(1) == 0)


def _params():
    return pltpu.CompilerParams(
        dimension_semantics=("arbitrary", "arbitrary"),
        vmem_limit_bytes=VMEM_LIMIT_BYTES)


def _cast_rows_per_step(n_rows, n_steps):
    rows = BF16_SUBLANES
    while n_rows % rows or n_rows // rows > n_steps:
        rows += BF16_SUBLANES
    return rows


def _cast_spec(shape, n_steps, n_seq_tiles):
    layers, n_rows, n_cols = shape
    rows = _cast_rows_per_step(n_rows, n_steps)
    last = n_rows // rows - 1

    def index(b, s):
        return (0, jnp.minimum(b * n_seq_tiles + s, last), 0)
    return pl.BlockSpec((layers, rows, n_cols), index)


def _conv_kernel(x_ref, g_ref, win_ref, cw_ref, wout_ref, wgu_ref, wdn_ref,
                 o_ref, wgu_bf_ref, wdn_bf_ref, h_scr, u_scr, z_scr, *, tm):
    cw = CONV_CHUNK
    wgu_bf_ref[...] = wgu_ref[...].astype(_BF16)
    wdn_bf_ref[...] = wdn_ref[...].astype(_BF16)

    @pl.when(pl.program_id(1) == 0)
    def _():
        u_scr[0:SUBLANES, :] = jnp.zeros((SUBLANES, D_MODEL), _F32)

    h_scr[...] = _rmsnorm_bf16(x_ref[...], g_ref[...])
    for j in range(D_MODEL // cw):
        cols = slice(j * cw, (j + 1) * cw)
        b_gate, c_gate, xv = [
            jnp.dot(h_scr[...], win_ref[:, part * D_MODEL + j * cw:part * D_MODEL + (j + 1) * cw],
                    preferred_element_type=_F32)
            for part in range(3)]
        u = b_gate * xv
        u_scr[SUBLANES:SUBLANES + tm, cols] = u
        w = cw_ref[:, cols]
        y = (w[2:3] * u
             + w[1:2] * u_scr[SUBLANES - 1:SUBLANES - 1 + tm, cols]
             + w[0:1] * u_scr[SUBLANES - 2:SUBLANES - 2 + tm, cols])
        z_scr[:, cols] = (c_gate * y).astype(_BF16)
        u_scr[0:SUBLANES, cols] = u_scr[tm:tm + SUBLANES, cols]
    mix = jnp.dot(z_scr[...], wout_ref[...], preferred_element_type=_F32)
    o_ref[...] = x_ref[...] + mix


def _conv_mixer(x, gain, w_in, conv_w, w_out, ffn_w_gate_up, ffn_w_down, tm):
    bsz, seq, d = x.shape
    n_seq_tiles = seq // tm
    n_steps = bsz * n_seq_tiles
    kern = functools.partial(_conv_kernel, tm=tm)
    return pl.pallas_call(
        kern,
        grid=(bsz, n_seq_tiles),
        in_specs=[
            _token_spec(tm),
            _const_spec((1, d)),
            _const_spec((d, 3 * d)),
            _const_spec((CONV_WIDTH, d)),
            _const_spec((d, d)),
            _cast_spec(ffn_w_gate_up.shape, n_steps, n_seq_tiles),
            _cast_spec(ffn_w_down.shape, n_steps, n_seq_tiles),
        ],
        out_specs=[
            _token_spec(tm),
            _cast_spec(ffn_w_gate_up.shape, n_steps, n_seq_tiles),
            _cast_spec(ffn_w_down.shape, n_steps, n_seq_tiles),
        ],
        out_shape=[
            jax.ShapeDtypeStruct(x.shape, _F32),
            jax.ShapeDtypeStruct(ffn_w_gate_up.shape, _BF16),
            jax.ShapeDtypeStruct(ffn_w_down.shape, _BF16),
        ],
        scratch_shapes=[
            pltpu.VMEM((tm, d), _BF16),
            pltpu.VMEM((tm + SUBLANES, d), _F32),
            pltpu.VMEM((tm, d), _BF16),
        ],
        compiler_params=_params(),
        name="conv_mixer",
    )(x, gain, w_in, conv_w, w_out, ffn_w_gate_up, ffn_w_down)


def _ffn_kernel(x_ref, xn_ref, g_ref, wgu_ref, wd_ref, o_ref, h_scr, a_scr):
    ck = FFN_CHUNK

    def gate_up_chunk(c):
        g = jnp.dot(h_scr[...], wgu_ref[:, c * ck:(c + 1) * ck],
                    preferred_element_type=_F32)
        u = jnp.dot(h_scr[...], wgu_ref[:, D_FF + c * ck:D_FF + (c + 1) * ck],
                    preferred_element_type=_F32)
        a_scr[:, c * ck:(c + 1) * ck] = (g * jax.nn.sigmoid(g) * u).astype(_BF16)

    @pl.when(_is_first_step())
    def _():
        h_scr[...] = _rmsnorm_bf16(x_ref[...], g_ref[...])
        gate_up_chunk(0)

    for c in range(1, D_FF // ck):
        gate_up_chunk(c)
    o_ref[...] = x_ref[...] + jnp.dot(a_scr[...], wd_ref[...],
                                      preferred_element_type=_F32)
    h_scr[...] = _rmsnorm_bf16(xn_ref[...], g_ref[...])
    gate_up_chunk(0)


def _layer_spec(shape, layer):
    return pl.BlockSpec((None,) + tuple(shape[1:]), lambda b, s: (layer, 0, 0),
                        pipeline_mode=pl.Buffered(1))


def _ffn(x, gain, w_gu, w_down, layer, tm):
    bsz, seq, d = x.shape
    return pl.pallas_call(
        _ffn_kernel,
        grid=(bsz, seq // tm),
        in_specs=[
            _token_spec(tm),
            _next_token_spec(tm, bsz, seq // tm),
            _const_spec((1, d)),
            _layer_spec(w_gu.shape, layer),
            _layer_spec(w_down.shape, layer),
        ],
        out_specs=_token_spec(tm),
        out_shape=jax.ShapeDtypeStruct(x.shape, _F32),
        scratch_shapes=[
            pltpu.VMEM((tm, d), _BF16),
            pltpu.VMEM((tm, D_FF), _BF16),
        ],
        compiler_params=_params(),
        name="swiglu_ffn",
    )(x, x, gain, w_gu, w_down)


def _pair_rmsnorm(t, gain):
    lo = lax.broadcasted_iota(jnp.int32, t.shape, 1) < HEAD_DIM
    sq = t * t
    s_lo = jnp.sum(jnp.where(lo, sq, 0.0), axis=-1, keepdims=True)
    s_hi = jnp.sum(jnp.where(lo, 0.0, sq), axis=-1, keepdims=True)
    ms = jnp.where(lo, s_lo, s_hi) * (1.0 / HEAD_DIM)
    return ((t * lax.rsqrt(ms + EPS)) * gain).astype(_BF16)


def _attn_kernel(x_ref, xn_ref, g_ref, wqkv_ref, qg_ref, kg_ref, bias_ref, sink_ref, wo_ref,
                 o_ref, h_scr, q_scr, k_scr, v_scr, a_scr, *, tm):
    blk = WINDOW
    d = D_MODEL
    kvw = N_KV_HEADS * HEAD_DIM
    part_rows = tm // ATTN_PARTS
    ncol = 256
    seq_tile = pl.program_id(1)

    @pl.when(seq_tile == 0)
    def _():
        k_scr[0:blk, :] = jnp.zeros((blk, kvw), _BF16)
        v_scr[0:blk, :] = jnp.zeros((blk, kvw), _BF16)

    def project_pieces(part, src_ref):
        r0 = part * part_rows
        rows = slice(r0, r0 + part_rows)

        def q_piece(c):
            if c == 0:
                h_scr[rows, :] = _rmsnorm_bf16(src_ref[rows, :], g_ref[...])
            q = jnp.dot(h_scr[rows, :], wqkv_ref[:, c * ncol:(c + 1) * ncol],
                        preferred_element_type=_F32)
            for j in range(ncol // LANES):
                lanes = slice(c * ncol + j * LANES, c * ncol + (j + 1) * LANES)
                q_scr[rows, lanes] = _pair_rmsnorm(q[:, j * LANES:(j + 1) * LANES],
                                                   qg_ref[...])

        def kv_piece():
            kv = jnp.dot(h_scr[rows, :], wqkv_ref[:, d:d + 2 * kvw],
                         preferred_element_type=_F32)
            for p in range(kvw // LANES):
                k_scr[blk + r0:blk + r0 + part_rows, p * LANES:(p + 1) * LANES] = (
                    _pair_rmsnorm(kv[:, p * LANES:(p + 1) * LANES], kg_ref[...]))
            v_scr[blk + r0:blk + r0 + part_rows, :] = kv[:, kvw:].astype(_BF16)

        return [functools.partial(q_piece, c) for c in range(d // ncol)] + [kv_piece]

    def output_pieces(part):
        rows = slice(part * part_rows, (part + 1) * part_rows)

        def o_piece(c):
            lanes = slice(c * ncol, (c + 1) * ncol)
            o_ref[rows, lanes] = x_ref[rows, lanes] + jnp.dot(
                a_scr[rows, :], wo_ref[:, lanes], preferred_element_type=_F32)

        return [functools.partial(o_piece, c) for c in range(d // ncol)]

    @pl.when(_is_first_step())
    def _():
        for piece in project_pieces(0, x_ref):
            piece()

    lo_lane = lax.broadcasted_iota(jnp.int32, (blk, LANES), 1) < HEAD_DIM
    lo_row = lax.broadcasted_iota(jnp.int32, (LANES, blk), 0) < HEAD_DIM
    cols = 2 * blk
    kc = lax.broadcasted_iota(jnp.int32, (blk, cols), 0)
    qi = lax.broadcasted_iota(jnp.int32, (blk, cols), 1) % blk
    in_cur = kc <= qi
    first_pen = jnp.where(seq_tile == 0, MASK_VALUE, 0.0)

    def scores(qb, pb):
        r0, p = qb * blk, pb // GROUP
        qp = q_scr[r0:r0 + blk, pb * LANES:(pb + 1) * LANES]
        zq = jnp.zeros_like(qp)
        qm = jnp.concatenate([jnp.where(lo_lane, qp, zq),
                              jnp.where(lo_lane, zq, qp)], axis=0)
        kw = k_scr[r0:r0 + 2 * blk, p * LANES:(p + 1) * LANES]
        s2 = lax.dot_general(kw, qm, (((1,), (1,)), ((), ())),
                             preferred_element_type=_F32)
        s_prev = s2[:blk] + first_pen if qb == 0 else s2[:blk]
        return jnp.where(in_cur, s2[blk:], s_prev)

    def softmax(qb, pb, s_pair):
        halves = []
        for hd in range(2):
            lanes = slice(hd * blk, (hd + 1) * blk)
            s = s_pair[:, lanes] + bias_ref[pb, :, lanes]
            sink = sink_ref[pb, :, lanes]
            m = jnp.maximum(jnp.max(s, axis=0, keepdims=True), sink)
            e = jnp.exp2(s - m)
            denom = jnp.sum(e, axis=0, keepdims=True) + jnp.exp2(sink - m)
            pr = (e * (1.0 / denom)).astype(_BF16)
            zero = jnp.zeros_like(pr)
            cur = in_cur[:, lanes]
            halves.append(jnp.concatenate([jnp.where(cur, zero, pr),
                                           jnp.where(cur, pr, zero)], axis=0))
        return jnp.concatenate(halves, axis=1)

    def weighted_values(qb, pb, p2):
        r0, p = qb * blk, pb // GROUP
        vw = v_scr[r0:r0 + 2 * blk, p * LANES:(p + 1) * LANES]
        o = lax.dot_general(vw, p2, (((0,), (0,)), ((), ())),
                            preferred_element_type=_F32)
        oj = jnp.where(lo_row, o[:, :blk], o[:, blk:])
        a_scr[r0:r0 + blk, pb * LANES:(pb + 1) * LANES] = oj.T.astype(_BF16)

    chains = [(qb, pb) for qb in range(tm // blk) for pb in range(N_Q_HEADS // 2)]
    part_chains = len(chains) // ATTN_PARTS
    extra = {}
    for part in range(1, ATTN_PARTS + 1):
        pieces = (project_pieces(part, x_ref) if part < ATTN_PARTS
                  else project_pieces(0, xn_ref))
        base = (part - 1) * part_chains + 1
        for n, piece in enumerate(pieces):
            extra.setdefault(base + n * (part_chains // len(pieces)), []).append(piece)
    for part in range(ATTN_PARTS - 1):
        pieces = output_pieces(part)
        base = (part + 1) * part_chains + 2 * ATTN_SKEW + 1
        room = part_chains - 2 * ATTN_SKEW - 1
        for n, piece in enumerate(pieces):
            extra.setdefault(base + n * (room // len(pieces)), []).append(piece)

    s2_vals, p2_vals = {}, {}
    for step in range(len(chains) + 2 * ATTN_SKEW):
        if step < len(chains):
            s2_vals[step] = scores(*chains[step])
        for piece in extra.get(step, ()):
            piece()
        i = step - ATTN_SKEW
        if 0 <= i < len(chains):
            p2_vals[i] = softmax(*chains[i], s2_vals.pop(i))
        i = step - 2 * ATTN_SKEW
        if 0 <= i < len(chains):
            weighted_values(*chains[i], p2_vals.pop(i))

    for piece in output_pieces(ATTN_PARTS - 1):
        piece()
    k_scr[0:blk, :] = k_scr[tm:tm + blk, :]
    v_scr[0:blk, :] = v_scr[tm:tm + blk, :]


def _attn_tables(sinks):
    perm = np.asarray(HEAD_PERM)
    slopes = np.exp2(-8.0 * np.arange(1, N_Q_HEADS + 1, dtype=np.float64) / N_Q_HEADS)
    kc = np.arange(WINDOW)[:, None]
    qi = np.arange(WINDOW)[None, :]
    dist = ((qi - kc) % WINDOW).astype(np.float64)
    bias = -slopes[perm][:, None, None] * dist[None] * LOG2E
    bias = bias.reshape(N_Q_HEADS // 2, 2, WINDOW, WINDOW).transpose(0, 2, 1, 3)
    bias = jnp.asarray(bias.reshape(N_Q_HEADS // 2, WINDOW, 2 * WINDOW), _F32)
    sink = jnp.repeat(sinks.astype(_F32)[perm] * LOG2E, WINDOW)
    return bias, sink.reshape(N_Q_HEADS // 2, 1, 2 * WINDOW)


def _attn_mixer(x, gain, w_qkv, q_gain, k_gain, bias, sink, w_o, tm):
    bsz, seq, d = x.shape
    kvw = N_KV_HEADS * HEAD_DIM
    pairs = N_Q_HEADS // 2
    cols = 2 * WINDOW
    kern = functools.partial(_attn_kernel, tm=tm)
    return pl.pallas_call(
        kern,
        grid=(bsz, seq // tm),
        in_specs=[
            _token_spec(tm),
            _next_token_spec(tm, bsz, seq // tm),
            _const_spec((1, d)),
            _const_spec((d, d + 2 * kvw)),
            _const_spec((1, LANES)),
            _const_spec((1, LANES)),
            _const_spec((pairs, WINDOW, cols)),
            _const_spec((pairs, 1, cols)),
            _const_spec((d, d)),
        ],
        out_specs=_token_spec(tm),
        out_shape=jax.ShapeDtypeStruct(x.shape, _F32),
        scratch_shapes=[
            pltpu.VMEM((tm, d), _BF16),
            pltpu.VMEM((tm, d), _BF16),
            pltpu.VMEM((tm + WINDOW, kvw), _BF16),
            pltpu.VMEM((tm + WINDOW, kvw), _BF16),
            pltpu.VMEM((tm, d), _BF16),
        ],
        compiler_params=_params(),
        name="swa_attention",
    )(x, x, gain, w_qkv, q_gain, k_gain, bias, sink, w_o)


def kernel(x, conv_w_in, conv_w, conv_w_out, attn_w_qkv, attn_q_gain, attn_k_gain,
           attn_sinks, attn_w_o, norm_mixer, norm_ffn, ffn_w_gate_up, ffn_w_down):
    tm = TOKEN_TILE
    d = D_MODEL
    perm = np.asarray(HEAD_PERM)

    x, w_gu, w_down = _conv_mixer(
        x, norm_mixer[0][None, :], conv_w_in[0].astype(_BF16), conv_w[0],
        conv_w_out[0].astype(_BF16), ffn_w_gate_up, ffn_w_down, tm)
    x = _ffn(x, norm_ffn[0][None, :], w_gu, w_down, 0, tm)

    w_qkv = attn_w_qkv[0].astype(_BF16)
    w_q = w_qkv[:, :d].reshape(d, N_Q_HEADS, HEAD_DIM)[:, perm].reshape(d, d)
    w_qkv = jnp.concatenate([w_q, w_qkv[:, d:]], axis=1)
    w_o = attn_w_o[0].astype(_BF16).reshape(N_Q_HEADS, HEAD_DIM, d)[perm].reshape(d, d)
    q_gain = jnp.tile(attn_q_gain[0], 2)[None, :] * (LOG2E * HEAD_DIM ** -0.5)
    k_gain = jnp.tile(attn_k_gain[0], 2)[None, :]
    bias, sink = _attn_tables(attn_sinks[0])
    x = _attn_mixer(x, norm_mixer[1][None, :], w_qkv, q_gain, k_gain, bias, sink, w_o, tm)
    x = _ffn(x, norm_ffn[1][None, :], w_gu, w_down, 1, tm)
    return x
```

```python
import functools

import numpy as np
import jax
import jax.numpy as jnp
from jax import lax
from jax.experimental import pallas as pl
from jax.experimental.pallas import tpu as pltpu

D_MODEL = 1024
HEAD_DIM = 64
N_Q_HEADS = 16
N_KV_HEADS = 4
GROUP = N_Q_HEADS // N_KV_HEADS
WINDOW = 128
CONV_WIDTH = 3
D_FF = 2816
EPS = 1e-6

LANES = 128
SUBLANES = 8
BF16_SUBLANES = 16
TOKEN_TILE = 1024
CONV_CHUNK = 256
FFN_CHUNK = 256
VMEM_LIMIT_BYTES = 56 * 1024 * 1024
MASK_VALUE = -1e30
ATTN_SKEW = 3
LOG2E = 1.4426950408889634
ATTN_PARTS = 4

HEAD_PERM = (0, 4, 1, 5, 2, 6, 3, 7, 8, 12, 9, 13, 10, 14, 11, 15)

_F32 = jnp.float32
_BF16 = jnp.bfloat16


def _rmsnorm_bf16(x, gain):
    ms = jnp.mean(x * x, axis=-1, keepdims=True)
    return ((x * lax.rsqrt(ms + EPS)) * gain).astype(_BF16)


def _const_spec(shape):
    return pl.BlockSpec(shape, lambda b, s: (0,) * len(shape),
                        pipeline_mode=pl.Buffered(1))


def _token_spec(tm):
    return pl.BlockSpec((None, tm, D_MODEL), lambda b, s: (b, s, 0))


def _next_token_spec(tm, bsz, n_seq_tiles):
    def index(b, s):
        nxt = jnp.minimum(b * n_seq_tiles + s + 1, bsz * n_seq_tiles - 1)
        return (nxt // n_seq_tiles, nxt % n_seq_tiles, 0)
    return pl.BlockSpec((None, tm, D_MODEL), index)


def _is_first_step():
    return jnp.logical_and(pl.program_id(0) == 0, pl.program_id(1) == 0)


def _params():
    return pltpu.CompilerParams(
        dimension_semantics=("arbitrary", "arbitrary"),
        vmem_limit_bytes=VMEM_LIMIT_BYTES)


def _cast_rows_per_step(n_rows, n_steps):
    rows = BF16_SUBLANES
    while n_rows % rows or n_rows // rows > n_steps:
        rows += BF16_SUBLANES
    return rows


def _cast_spec(shape, n_steps, n_seq_tiles):
    layers, n_rows, n_cols = shape
    rows = _cast_rows_per_step(n_rows, n_steps)
    last = n_rows // rows - 1

    def index(b, s):
        return (0, jnp.minimum(b * n_seq_tiles + s, last), 0)
    return pl.BlockSpec((layers, rows, n_cols), index)


def _conv_kernel(x_ref, g_ref, win_ref, cw_ref, wout_ref, wgu_ref, wdn_ref,
                 o_ref, wgu_bf_ref, wdn_bf_ref, h_scr, u_scr, z_scr, *, tm):
    cw = CONV_CHUNK
    wgu_bf_ref[...] = wgu_ref[...].astype(_BF16)
    wdn_bf_ref[...] = wdn_ref[...].astype(_BF16)

    @pl.when(pl.program_id(1) == 0)
    def _():
        u_scr[0:SUBLANES, :] = jnp.zeros((SUBLANES, D_MODEL), _F32)

    h_scr[...] = _rmsnorm_bf16(x_ref[...], g_ref[...])
    for j in range(D_MODEL // cw):
        cols = slice(j * cw, (j + 1) * cw)
        b_gate, c_gate, xv = [
            jnp.dot(h_scr[...], win_ref[:, part * D_MODEL + j * cw:part * D_MODEL + (j + 1) * cw],
                    preferred_element_type=_F32)
            for part in range(3)]
        u = b_gate * xv
        u_scr[SUBLANES:SUBLANES + tm, cols] = u
        w = cw_ref[:, cols]
        y = (w[2:3] * u
             + w[1:2] * u_scr[SUBLANES - 1:SUBLANES - 1 + tm, cols]
             + w[0:1] * u_scr[SUBLANES - 2:SUBLANES - 2 + tm, cols])
        z_scr[:, cols] = (c_gate * y).astype(_BF16)
        u_scr[0:SUBLANES, cols] = u_scr[tm:tm + SUBLANES, cols]
    mix = jnp.dot(z_scr[...], wout_ref[...], preferred_element_type=_F32)
    o_ref[...] = x_ref[...] + mix


def _conv_mixer(x, gain, w_in, conv_w, w_out, ffn_w_gate_up, ffn_w_down, tm):
    bsz, seq, d = x.shape
    n_seq_tiles = seq // tm
    n_steps = bsz * n_seq_tiles
    kern = functools.partial(_conv_kernel, tm=tm)
    return pl.pallas_call(
        kern,
        grid=(bsz, n_seq_tiles),
        in_specs=[
            _token_spec(tm),
            _const_spec((1, d)),
            _const_spec((d, 3 * d)),
            _const_spec((CONV_WIDTH, d)),
            _const_spec((d, d)),
            _cast_spec(ffn_w_gate_up.shape, n_steps, n_seq_tiles),
            _cast_spec(ffn_w_down.shape, n_steps, n_seq_tiles),
        ],
        out_specs=[
            _token_spec(tm),
            _cast_spec(ffn_w_gate_up.shape, n_steps, n_seq_tiles),
            _cast_spec(ffn_w_down.shape, n_steps, n_seq_tiles),
        ],
        out_shape=[
            jax.ShapeDtypeStruct(x.shape, _F32),
            jax.ShapeDtypeStruct(ffn_w_gate_up.shape, _BF16),
            jax.ShapeDtypeStruct(ffn_w_down.shape, _BF16),
        ],
        scratch_shapes=[
            pltpu.VMEM((tm, d), _BF16),
            pltpu.VMEM((tm + SUBLANES, d), _F32),
            pltpu.VMEM((tm, d), _BF16),
        ],
        compiler_params=_params(),
        name="conv_mixer",
    )(x, gain, w_in, conv_w, w_out, ffn_w_gate_up, ffn_w_down)


def _ffn_kernel(x_ref, xn_ref, g_ref, wgu_ref, wd_ref, o_ref, h_scr, a_scr):
    ck = FFN_CHUNK

    def gate_up_chunk(c):
        g = jnp.dot(h_scr[...], wgu_ref[:, c * ck:(c + 1) * ck],
                    preferred_element_type=_F32)
        u = jnp.dot(h_scr[...], wgu_ref[:, D_FF + c * ck:D_FF + (c + 1) * ck],
                    preferred_element_type=_F32)
        a_scr[:, c * ck:(c + 1) * ck] = (g * jax.nn.sigmoid(g) * u).astype(_BF16)

    @pl.when(_is_first_step())
    def _():
        h_scr[...] = _rmsnorm_bf16(x_ref[...], g_ref[...])
        gate_up_chunk(0)

    for c in range(1, D_FF // ck):
        gate_up_chunk(c)
    o_ref[...] = x_ref[...] + jnp.dot(a_scr[...], wd_ref[...],
                                      preferred_element_type=_F32)
    h_scr[...] = _rmsnorm_bf16(xn_ref[...], g_ref[...])
    gate_up_chunk(0)


def _layer_spec(shape, layer):
    return pl.BlockSpec((None,) + tuple(shape[1:]), lambda b, s: (layer, 0, 0),
                        pipeline_mode=pl.Buffered(1))


def _ffn(x, gain, w_gu, w_down, layer, tm):
    bsz, seq, d = x.shape
    return pl.pallas_call(
        _ffn_kernel,
        grid=(bsz, seq // tm),
        in_specs=[
            _token_spec(tm),
            _next_token_spec(tm, bsz, seq // tm),
            _const_spec((1, d)),
            _layer_spec(w_gu.shape, layer),
            _layer_spec(w_down.shape, layer),
        ],
        out_specs=_token_spec(tm),
        out_shape=jax.ShapeDtypeStruct(x.shape, _F32),
        scratch_shapes=[
            pltpu.VMEM((tm, d), _BF16),
            pltpu.VMEM((tm, D_FF), _BF16),
        ],
        compiler_params=_params(),
        name="swiglu_ffn",
    )(x, x, gain, w_gu, w_down)


def _pair_rmsnorm(t, gain):
    lo = lax.broadcasted_iota(jnp.int32, t.shape, 1) < HEAD_DIM
    sq = t * t
    s_lo = jnp.sum(jnp.where(lo, sq, 0.0), axis=-1, keepdims=True)
    s_hi = jnp.sum(jnp.where(lo, 0.0, sq), axis=-1, keepdims=True)
    r = lax.rsqrt(jnp.where(lo, s_lo, s_hi) + HEAD_DIM * EPS)
    return ((t * r) * gain).astype(_BF16)


def _attn_kernel(x_ref, xn_ref, g_ref, wqkv_ref, qg_ref, kg_ref, bias_ref, sink_ref, wo_ref,
                 o_ref, h_scr, q_scr, k_scr, v_scr, a_scr, *, tm):
    blk = WINDOW
    d = D_MODEL
    kvw = N_KV_HEADS * HEAD_DIM
    part_rows = tm // ATTN_PARTS
    ncol = 256
    seq_tile = pl.program_id(1)

    @pl.when(seq_tile == 0)
    def _():
        k_scr[0:blk, :] = jnp.zeros((blk, kvw), _BF16)
        v_scr[0:blk, :] = jnp.zeros((blk, kvw), _BF16)

    def project_pieces(part, src_ref):
        r0 = part * part_rows
        rows = slice(r0, r0 + part_rows)

        def q_piece(c):
            if c == 0:
                h_scr[rows, :] = _rmsnorm_bf16(src_ref[rows, :], g_ref[...])
            q = jnp.dot(h_scr[rows, :], wqkv_ref[:, c * ncol:(c + 1) * ncol],
                        preferred_element_type=_F32)
            for j in range(ncol // LANES):
                lanes = slice(c * ncol + j * LANES, c * ncol + (j + 1) * LANES)
                q_scr[rows, lanes] = _pair_rmsnorm(q[:, j * LANES:(j + 1) * LANES],
                                                   qg_ref[...])

        def kv_piece():
            kv = jnp.dot(h_scr[rows, :], wqkv_ref[:, d:d + 2 * kvw],
                         preferred_element_type=_F32)
            for p in range(kvw // LANES):
                k_scr[blk + r0:blk + r0 + part_rows, p * LANES:(p + 1) * LANES] = (
                    _pair_rmsnorm(kv[:, p * LANES:(p + 1) * LANES], kg_ref[...]))
            v_scr[blk + r0:blk + r0 + part_rows, :] = kv[:, kvw:].astype(_BF16)

        return [functools.partial(q_piece, c) for c in range(d // ncol)] + [kv_piece]

    def output_pieces(part):
        rows = slice(part * part_rows, (part + 1) * part_rows)

        def o_piece(c):
            lanes = slice(c * ncol, (c + 1) * ncol)
            o_ref[rows, lanes] = x_ref[rows, lanes] + jnp.dot(
                a_scr[rows, :], wo_ref[:, lanes], preferred_element_type=_F32)

        return [functools.partial(o_piece, c) for c in range(d // ncol)]

    @pl.when(_is_first_step())
    def _():
        for piece in project_pieces(0, x_ref):
            piece()

    lo_lane = lax.broadcasted_iota(jnp.int32, (blk, LANES), 1) < HEAD_DIM
    lo_row = lax.broadcasted_iota(jnp.int32, (LANES, blk), 0) < HEAD_DIM
    cols = 2 * blk
    kc = lax.broadcasted_iota(jnp.int32, (blk, cols), 0)
    qi = lax.broadcasted_iota(jnp.int32, (blk, cols), 1) % blk
    in_cur = kc <= qi
    first_pen = jnp.where(seq_tile == 0, MASK_VALUE, 0.0)

    def scores(qb, pb):
        r0, p = qb * blk, pb // GROUP
        qp = q_scr[r0:r0 + blk, pb * LANES:(pb + 1) * LANES]
        zq = jnp.zeros_like(qp)
        qm = jnp.concatenate([jnp.where(lo_lane, qp, zq),
                              jnp.where(lo_lane, zq, qp)], axis=0)
        kw = k_scr[r0:r0 + 2 * blk, p * LANES:(p + 1) * LANES]
        s2 = lax.dot_general(kw, qm, (((1,), (1,)), ((), ())),
                             preferred_element_type=_F32)
        s_prev = s2[:blk] + first_pen if qb == 0 else s2[:blk]
        return jnp.where(in_cur, s2[blk:], s_prev)

    def softmax(qb, pb, s_pair):
        halves = []
        for hd in range(2):
            lanes = slice(hd * blk, (hd + 1) * blk)
            s = s_pair[:, lanes] + bias_ref[pb, :, lanes]
            sink = sink_ref[pb, :, lanes]
            m = jnp.maximum(jnp.max(s, axis=0, keepdims=True), sink)
            e = jnp.exp2(s - m)
            denom = jnp.sum(e, axis=0, keepdims=True) + jnp.exp2(sink - m)
            pr = (e * (1.0 / denom)).astype(_BF16)
            zero = jnp.zeros_like(pr)
            cur = in_cur[:, lanes]
            halves.append(jnp.concatenate([jnp.where(cur, zero, pr),
                                           jnp.where(cur, pr, zero)], axis=0))
        return jnp.concatenate(halves, axis=1)

    def weighted_values(qb, pb, p2):
        r0, p = qb * blk, pb // GROUP
        vw = v_scr[r0:r0 + 2 * blk, p * LANES:(p + 1) * LANES]
        o = lax.dot_general(vw, p2, (((0,), (0,)), ((), ())),
                            preferred_element_type=_F32)
        oj = jnp.where(lo_row, o[:, :blk], o[:, blk:])
        a_scr[r0:r0 + blk, pb * LANES:(pb + 1) * LANES] = oj.T.astype(_BF16)

    chains = [(qb, pb) for qb in range(tm // blk) for pb in range(N_Q_HEADS // 2)]
    part_chains = len(chains) // ATTN_PARTS
    extra = {}
    for part in range(1, ATTN_PARTS + 1):
        pieces = (project_pieces(part, x_ref) if part < ATTN_PARTS
                  else project_pieces(0, xn_ref))
        base = (part - 1) * part_chains + 1
        for n, piece in enumerate(pieces):
            extra.setdefault(base + n * (part_chains // len(pieces)), []).append(piece)
    for part in range(ATTN_PARTS - 1):
        pieces = output_pieces(part)
        base = (part + 1) * part_chains + 2 * ATTN_SKEW + 1
        room = part_chains - 2 * ATTN_SKEW - 1
        for n, piece in enumerate(pieces):
            extra.setdefault(base + n * (room // len(pieces)), []).append(piece)

    s2_vals, p2_vals = {}, {}
    for step in range(len(chains) + 2 * ATTN_SKEW):
        if step < len(chains):
            s2_vals[step] = scores(*chains[step])
        for piece in extra.get(step, ()):
            piece()
        i = step - ATTN_SKEW
        if 0 <= i < len(chains):
            p2_vals[i] = softmax(*chains[i], s2_vals.pop(i))
        i = step - 2 * ATTN_SKEW
        if 0 <= i < len(chains):
            weighted_values(*chains[i], p2_vals.pop(i))

    for piece in output_pieces(ATTN_PARTS - 1):
        piece()
    k_scr[0:blk, :] = k_scr[tm:tm + blk, :]
    v_scr[0:blk, :] = v_scr[tm:tm + blk, :]


def _attn_tables(sinks):
    perm = np.asarray(HEAD_PERM)
    slopes = np.exp2(-8.0 * np.arange(1, N_Q_HEADS + 1, dtype=np.float64) / N_Q_HEADS)
    kc = np.arange(WINDOW)[:, None]
    qi = np.arange(WINDOW)[None, :]
    dist = ((qi - kc) % WINDOW).astype(np.float64)
    bias = -slopes[perm][:, None, None] * dist[None] * LOG2E
    bias = bias.reshape(N_Q_HEADS // 2, 2, WINDOW, WINDOW).transpose(0, 2, 1, 3)
    bias = jnp.asarray(bias.reshape(N_Q_HEADS // 2, WINDOW, 2 * WINDOW), _F32)
    sink = jnp.repeat(sinks.astype(_F32)[perm] * LOG2E, WINDOW)
    return bias, sink.reshape(N_Q_HEADS // 2, 1, 2 * WINDOW)


def _attn_mixer(x, gain, w_qkv, q_gain, k_gain, bias, sink, w_o, tm):
    bsz, seq, d = x.shape
    kvw = N_KV_HEADS * HEAD_DIM
    pairs = N_Q_HEADS // 2
    cols = 2 * WINDOW
    kern = functools.partial(_attn_kernel, tm=tm)
    return pl.pallas_call(
        kern,
        grid=(bsz, seq // tm),
        in_specs=[
            _token_spec(tm),
            _next_token_spec(tm, bsz, seq // tm),
            _const_spec((1, d)),
            _const_spec((d, d + 2 * kvw)),
            _const_spec((1, LANES)),
            _const_spec((1, LANES)),
            _const_spec((pairs, WINDOW, cols)),
            _const_spec((pairs, 1, cols)),
            _const_spec((d, d)),
        ],
        out_specs=_token_spec(tm),
        out_shape=jax.ShapeDtypeStruct(x.shape, _F32),
        scratch_shapes=[
            pltpu.VMEM((tm, d), _BF16),
            pltpu.VMEM((tm, d), _BF16),
            pltpu.VMEM((tm + WINDOW, kvw), _BF16),
            pltpu.VMEM((tm + WINDOW, kvw), _BF16),
            pltpu.VMEM((tm, d), _BF16),
        ],
        compiler_params=_params(),
        name="swa_attention",
    )(x, x, gain, w_qkv, q_gain, k_gain, bias, sink, w_o)


def kernel(x, conv_w_in, conv_w, conv_w_out, attn_w_qkv, attn_q_gain, attn_k_gain,
           attn_sinks, attn_w_o, norm_mixer, norm_ffn, ffn_w_gate_up, ffn_w_down):
    tm = TOKEN_TILE
    d = D_MODEL
    perm = np.asarray(HEAD_PERM)

    x, w_gu, w_down = _conv_mixer(
        x, norm_mixer[0][None, :], conv_w_in[0].astype(_BF16), conv_w[0],
        conv_w_out[0].astype(_BF16), ffn_w_gate_up, ffn_w_down, tm)
    x = _ffn(x, norm_ffn[0][None, :], w_gu, w_down, 0, tm)

    w_qkv = attn_w_qkv[0].astype(_BF16)
    w_q = w_qkv[:, :d].reshape(d, N_Q_HEADS, HEAD_DIM)[:, perm].reshape(d, d)
    w_qkv = jnp.concatenate([w_q, w_qkv[:, d:]], axis=1)
    w_o = attn_w_o[0].astype(_BF16).reshape(N_Q_HEADS, HEAD_DIM, d)[perm].reshape(d, d)
    q_gain = jnp.tile(attn_q_gain[0], 2)[None, :] * LOG2E
    k_gain = jnp.tile(attn_k_gain[0], 2)[None, :] * HEAD_DIM ** 0.5
    bias, sink = _attn_tables(attn_sinks[0])
    x = _attn_mixer(x, norm_mixer[1][None, :], w_qkv, q_gain, k_gain, bias, sink, w_o, tm)
    x = _ffn(x, norm_ffn[1][None, :], w_gu, w_down, 1, tm)
    return x
```

```python
import functools

import numpy as np
import jax
import jax.numpy as jnp
from jax import lax
from jax.experimental import pallas as pl
from jax.experimental.pallas import tpu as pltpu

D_MODEL = 1024
HEAD_DIM = 64
N_Q_HEADS = 16
N_KV_HEADS = 4
GROUP = N_Q_HEADS // N_KV_HEADS
WINDOW = 128
CONV_WIDTH = 3
D_FF = 2816
EPS = 1e-6

LANES = 128
SUBLANES = 8
BF16_SUBLANES = 16
TOKEN_TILE = 1024
CONV_CHUNK = 256
FFN_CHUNK = 256
VMEM_LIMIT_BYTES = 56 * 1024 * 1024
MASK_VALUE = -1e30
LOG2E = 1.4426950408889634
ATTN_PARTS = 4
ATTN_COL_CHUNK = 256
ATTN_SKEW = 3
ATTN_PROJ_OFFSETS = (0, 3, 6, 10, 14)
ATTN_OUT_OFFSETS = (8, 12, 18, 21)

HEAD_PERM = (0, 4, 1, 5, 2, 6, 3, 7, 8, 12, 9, 13, 10, 14, 11, 15)

_F32 = jnp.float32
_BF16 = jnp.bfloat16


def _rmsnorm_bf16(x, gain):
    ms = jnp.mean(x * x, axis=-1, keepdims=True)
    return ((x * lax.rsqrt(ms + EPS)) * gain).astype(_BF16)


def _const_spec(shape):
    return pl.BlockSpec(shape, lambda b, s: (0,) * len(shape),
                        pipeline_mode=pl.Buffered(1))


def _token_spec(tm):
    return pl.BlockSpec((None, tm, D_MODEL), lambda b, s: (b, s, 0))


def _next_token_spec(tm, bsz, n_seq_tiles, rows=None):
    rows = tm if rows is None else rows

    def index(b, s):
        nxt = jnp.minimum(b * n_seq_tiles + s + 1, bsz * n_seq_tiles - 1)
        return (nxt // n_seq_tiles, (nxt % n_seq_tiles) * (tm // rows), 0)
    return pl.BlockSpec((None, rows, D_MODEL), index)


def _is_first_step():
    return jnp.logical_and(pl.program_id(0) == 0, pl.program_id(1) == 0)


def _params():
    return pltpu.CompilerParams(
        dimension_semantics=("arbitrary", "arbitrary"),
        vmem_limit_bytes=VMEM_LIMIT_BYTES)


def _cast_rows_per_step(n_rows, n_steps):
    rows = BF16_SUBLANES
    while n_rows % rows or n_rows // rows > n_steps:
        rows += BF16_SUBLANES
    return rows


def _cast_spec(shape, n_steps, n_seq_tiles):
    layers, n_rows, n_cols = shape
    rows = _cast_rows_per_step(n_rows, n_steps)
    last = n_rows // rows - 1

    def index(b, s):
        return (0, jnp.minimum(b * n_seq_tiles + s, last), 0)
    return pl.BlockSpec((layers, rows, n_cols), index)


def _conv_kernel(x_ref, g_ref, win_ref, cw_ref, wout_ref, wgu_ref, wdn_ref,
                 o_ref, wgu_bf_ref, wdn_bf_ref, h_scr, u_scr, z_scr, *, tm):
    cw = CONV_CHUNK
    wgu_bf_ref[...] = wgu_ref[...].astype(_BF16)
    wdn_bf_ref[...] = wdn_ref[...].astype(_BF16)

    @pl.when(pl.program_id(1) == 0)
    def _():
        u_scr[0:SUBLANES, :] = jnp.zeros((SUBLANES, D_MODEL), _F32)

    h_scr[...] = _rmsnorm_bf16(x_ref[...], g_ref[...])
    for j in range(D_MODEL // cw):
        cols = slice(j * cw, (j + 1) * cw)
        b_gate, c_gate, xv = [
            jnp.dot(h_scr[...], win_ref[:, part * D_MODEL + j * cw:part * D_MODEL + (j + 1) * cw],
                    preferred_element_type=_F32)
            for part in range(3)]
        u = b_gate * xv
        u_scr[SUBLANES:SUBLANES + tm, cols] = u
        w = cw_ref[:, cols]
        y = (w[2:3] * u
             + w[1:2] * u_scr[SUBLANES - 1:SUBLANES - 1 + tm, cols]
             + w[0:1] * u_scr[SUBLANES - 2:SUBLANES - 2 + tm, cols])
        z_scr[:, cols] = (c_gate * y).astype(_BF16)
        u_scr[0:SUBLANES, cols] = u_scr[tm:tm + SUBLANES, cols]
    mix = jnp.dot(z_scr[...], wout_ref[...], preferred_element_type=_F32)
    o_ref[...] = x_ref[...] + mix


def _conv_mixer(x, gain, w_in, conv_w, w_out, ffn_w_gate_up, ffn_w_down, tm):
    bsz, seq, d = x.shape
    n_seq_tiles = seq // tm
    n_steps = bsz * n_seq_tiles
    kern = functools.partial(_conv_kernel, tm=tm)
    return pl.pallas_call(
        kern,
        grid=(bsz, n_seq_tiles),
        in_specs=[
            _token_spec(tm),
            _const_spec((1, d)),
            _const_spec((d, 3 * d)),
            _const_spec((CONV_WIDTH, d)),
            _const_spec((d, d)),
            _cast_spec(ffn_w_gate_up.shape, n_steps, n_seq_tiles),
            _cast_spec(ffn_w_down.shape, n_steps, n_seq_tiles),
        ],
        out_specs=[
            _token_spec(tm),
            _cast_spec(ffn_w_gate_up.shape, n_steps, n_seq_tiles),
            _cast_spec(ffn_w_down.shape, n_steps, n_seq_tiles),
        ],
        out_shape=[
            jax.ShapeDtypeStruct(x.shape, _F32),
            jax.ShapeDtypeStruct(ffn_w_gate_up.shape, _BF16),
            jax.ShapeDtypeStruct(ffn_w_down.shape, _BF16),
        ],
        scratch_shapes=[
            pltpu.VMEM((tm, d), _BF16),
            pltpu.VMEM((tm + SUBLANES, d), _F32),
            pltpu.VMEM((tm, d), _BF16),
        ],
        compiler_params=_params(),
        name="conv_mixer",
    )(x, gain, w_in, conv_w, w_out, ffn_w_gate_up, ffn_w_down)


def _ffn_kernel(x_ref, xn_ref, g_ref, wgu_ref, wd_ref, o_ref, h_scr, a_scr):
    ck = FFN_CHUNK

    def gate_up_chunk(c):
        g = jnp.dot(h_scr[...], wgu_ref[:, c * ck:(c + 1) * ck],
                    preferred_element_type=_F32)
        u = jnp.dot(h_scr[...], wgu_ref[:, D_FF + c * ck:D_FF + (c + 1) * ck],
                    preferred_element_type=_F32)
        a_scr[:, c * ck:(c + 1) * ck] = (g * jax.nn.sigmoid(g) * u).astype(_BF16)

    @pl.when(_is_first_step())
    def _():
        h_scr[...] = _rmsnorm_bf16(x_ref[...], g_ref[...])
        gate_up_chunk(0)

    for c in range(1, D_FF // ck):
        gate_up_chunk(c)
    o_ref[...] = x_ref[...] + jnp.dot(a_scr[...], wd_ref[...],
                                      preferred_element_type=_F32)
    h_scr[...] = _rmsnorm_bf16(xn_ref[...], g_ref[...])
    gate_up_chunk(0)


def _layer_spec(shape, layer):
    return pl.BlockSpec((None,) + tuple(shape[1:]), lambda b, s: (layer, 0, 0),
                        pipeline_mode=pl.Buffered(1))


def _ffn(x, gain, w_gu, w_down, layer, tm):
    bsz, seq, d = x.shape
    return pl.pallas_call(
        _ffn_kernel,
        grid=(bsz, seq // tm),
        in_specs=[
            _token_spec(tm),
            _next_token_spec(tm, bsz, seq // tm),
            _const_spec((1, d)),
            _layer_spec(w_gu.shape, layer),
            _layer_spec(w_down.shape, layer),
        ],
        out_specs=_token_spec(tm),
        out_shape=jax.ShapeDtypeStruct(x.shape, _F32),
        scratch_shapes=[
            pltpu.VMEM((tm, d), _BF16),
            pltpu.VMEM((tm, D_FF), _BF16),
        ],
        compiler_params=_params(),
        name="swiglu_ffn",
    )(x, x, gain, w_gu, w_down)


def _pair_rmsnorm(t, gain):
    lo = lax.broadcasted_iota(jnp.int32, t.shape, 1) < HEAD_DIM
    sq = t * t
    s_lo = jnp.sum(jnp.where(lo, sq, 0.0), axis=-1, keepdims=True)
    s_hi = jnp.sum(jnp.where(lo, 0.0, sq), axis=-1, keepdims=True)
    r = lax.rsqrt(jnp.where(lo, s_lo, s_hi) + HEAD_DIM * EPS)
    return ((t * r) * gain).astype(_BF16)


def _pair_rmsnorm_t(t, gain):
    halves = []
    for hd in range(2):
        th = t[hd * HEAD_DIM:(hd + 1) * HEAD_DIM]
        ss = jnp.sum(th * th, axis=0, keepdims=True)
        halves.append(th * lax.rsqrt(ss + HEAD_DIM * EPS))
    return (jnp.concatenate(halves, axis=0) * gain).astype(_BF16)


def _attn_kernel(x_ref, xn_ref, g_ref, wqt_ref, wk_ref, wvt_ref, qg_ref, kg_ref, bias_ref,
                 sink_ref, wo_ref, o_ref, h_scr, qt_scr, k_scr, vt_scr, a_scr, *, tm):
    blk = WINDOW
    d = D_MODEL
    kvw = N_KV_HEADS * HEAD_DIM
    part_rows = tm // ATTN_PARTS
    ncol = ATTN_COL_CHUNK
    seq_tile = pl.program_id(1)

    @pl.when(seq_tile == 0)
    def _():
        k_scr[0:blk, :] = jnp.zeros((blk, kvw), _BF16)
        vt_scr[:, 0:blk] = jnp.zeros((kvw, blk), _BF16)

    def project_pieces(part, src_ref):
        r0 = part * part_rows
        rows = slice(r0, r0 + part_rows)

        def q_piece(c):
            if c == 0:
                h_scr[rows, :] = _rmsnorm_bf16(src_ref[rows, :], g_ref[...])
            qt = lax.dot_general(wqt_ref[c * ncol:(c + 1) * ncol, :], h_scr[rows, :],
                                 (((1,), (1,)), ((), ())),
                                 preferred_element_type=_F32)
            for j in range(ncol // LANES):
                dims = slice(c * ncol + j * LANES, c * ncol + (j + 1) * LANES)
                qt_scr[dims, rows] = _pair_rmsnorm_t(qt[j * LANES:(j + 1) * LANES],
                                                     qg_ref[...])

        def kv_piece():
            k = jnp.dot(h_scr[rows, :], wk_ref[...], preferred_element_type=_F32)
            for p in range(kvw // LANES):
                k_scr[blk + r0:blk + r0 + part_rows, p * LANES:(p + 1) * LANES] = (
                    _pair_rmsnorm(k[:, p * LANES:(p + 1) * LANES], kg_ref[...]))
            vt = lax.dot_general(wvt_ref[...], h_scr[rows, :], (((1,), (1,)), ((), ())),
                                 preferred_element_type=_F32)
            vt_scr[:, blk + r0:blk + r0 + part_rows] = vt.astype(_BF16)

        return [functools.partial(q_piece, c) for c in range(d // ncol)] + [kv_piece]

    def output_pieces(part):
        rows = slice(part * part_rows, (part + 1) * part_rows)

        def o_piece(c):
            lanes = slice(c * ncol, (c + 1) * ncol)
            o_ref[rows, lanes] = x_ref[rows, lanes] + jnp.dot(
                a_scr[rows, :], wo_ref[:, lanes], preferred_element_type=_F32)

        return [functools.partial(o_piece, c) for c in range(d // ncol)]

    @pl.when(_is_first_step())
    def _():
        for piece in project_pieces(0, x_ref):
            piece()

    lo_row = lax.broadcasted_iota(jnp.int32, (LANES, blk), 0) < HEAD_DIM
    cols = 2 * blk
    kc = lax.broadcasted_iota(jnp.int32, (blk, cols), 0)
    qi = lax.broadcasted_iota(jnp.int32, (blk, cols), 1) % blk
    in_cur = kc <= qi
    first_pen = jnp.where(seq_tile == 0, MASK_VALUE, 0.0)

    def scores(qb, pb):
        r0, p = qb * blk, pb // GROUP
        qp = qt_scr[pb * LANES:(pb + 1) * LANES, r0:r0 + blk]
        zq = jnp.zeros_like(qp)
        qm = jnp.concatenate([jnp.where(lo_row, qp, zq),
                              jnp.where(lo_row, zq, qp)], axis=1)
        kw = k_scr[r0:r0 + 2 * blk, p * LANES:(p + 1) * LANES]
        s2 = jnp.dot(kw, qm, preferred_element_type=_F32)
        s_prev = s2[:blk] + first_pen if qb == 0 else s2[:blk]
        return jnp.where(in_cur, s2[blk:], s_prev)

    def softmax(qb, pb, s_pair):
        halves = []
        for hd in range(2):
            lanes = slice(hd * blk, (hd + 1) * blk)
            s = s_pair[:, lanes] + bias_ref[pb, :, lanes]
            sink = sink_ref[pb, :, lanes]
            m = jnp.maximum(jnp.max(s, axis=0, keepdims=True), sink)
            e = jnp.exp2(s - m)
            denom = jnp.sum(e, axis=0, keepdims=True) + jnp.exp2(sink - m)
            pr = (e * (1.0 / denom)).astype(_BF16)
            zero = jnp.zeros_like(pr)
            cur = in_cur[:, lanes]
            halves.append(jnp.concatenate([jnp.where(cur, zero, pr),
                                           jnp.where(cur, pr, zero)], axis=0))
        return jnp.concatenate(halves, axis=1)

    def weighted_values(qb, pb, p2):
        r0, p = qb * blk, pb // GROUP
        vtw = vt_scr[p * LANES:(p + 1) * LANES, r0:r0 + 2 * blk]
        o = jnp.dot(vtw, p2, preferred_element_type=_F32)
        oj = jnp.where(lo_row, o[:, :blk], o[:, blk:])
        a_scr[r0:r0 + blk, pb * LANES:(pb + 1) * LANES] = oj.T.astype(_BF16)

    chains = [(qb, pb) for qb in range(tm // blk) for pb in range(N_Q_HEADS // 2)]
    part_chains = len(chains) // ATTN_PARTS
    extra = {}
    for part in range(1, ATTN_PARTS + 1):
        pieces = (project_pieces(part, x_ref) if part < ATTN_PARTS
                  else project_pieces(0, xn_ref))
        base = (part - 1) * part_chains
        for offset, piece in zip(ATTN_PROJ_OFFSETS, pieces, strict=True):
            assert offset < part_chains
            extra.setdefault(base + offset, []).append(piece)
    for part in range(ATTN_PARTS - 1):
        pieces = output_pieces(part)
        base = (part + 1) * part_chains
        for offset, piece in zip(ATTN_OUT_OFFSETS, pieces, strict=True):
            assert 2 * ATTN_SKEW < offset and base + offset < len(chains) + 2 * ATTN_SKEW
            extra.setdefault(base + offset, []).append(piece)

    s2_vals, p2_vals = {}, {}
    for step in range(len(chains) + 2 * ATTN_SKEW):
        if step < len(chains):
            s2_vals[step] = scores(*chains[step])
        for piece in extra.get(step, ()):
            piece()
        i = step - ATTN_SKEW
        if 0 <= i < len(chains):
            p2_vals[i] = softmax(*chains[i], s2_vals.pop(i))
        i = step - 2 * ATTN_SKEW
        if 0 <= i < len(chains):
            weighted_values(*chains[i], p2_vals.pop(i))

    for piece in output_pieces(ATTN_PARTS - 1):
        piece()
    k_scr[0:blk, :] = k_scr[tm:tm + blk, :]
    vt_scr[:, 0:blk] = vt_scr[:, tm:tm + blk]


def _attn_tables(sinks):
    perm = np.asarray(HEAD_PERM)
    slopes = np.exp2(-8.0 * np.arange(1, N_Q_HEADS + 1, dtype=np.float64) / N_Q_HEADS)
    kc = np.arange(WINDOW)[:, None]
    qi = np.arange(WINDOW)[None, :]
    dist = ((qi - kc) % WINDOW).astype(np.float64)
    bias = -slopes[perm][:, None, None] * dist[None] * LOG2E
    bias = bias.reshape(N_Q_HEADS // 2, 2, WINDOW, WINDOW).transpose(0, 2, 1, 3)
    bias = jnp.asarray(bias.reshape(N_Q_HEADS // 2, WINDOW, 2 * WINDOW), _F32)
    sink = jnp.repeat(sinks.astype(_F32)[perm] * LOG2E, WINDOW)
    return bias, sink.reshape(N_Q_HEADS // 2, 1, 2 * WINDOW)


def _attn_mixer(x, gain, w_qt, w_k, w_vt, q_gain, k_gain, bias, sink, w_o, tm):
    bsz, seq, d = x.shape
    kvw = N_KV_HEADS * HEAD_DIM
    pairs = N_Q_HEADS // 2
    cols = 2 * WINDOW
    kern = functools.partial(_attn_kernel, tm=tm)
    return pl.pallas_call(
        kern,
        grid=(bsz, seq // tm),
        in_specs=[
            _token_spec(tm),
            _next_token_spec(tm, bsz, seq // tm, rows=tm // ATTN_PARTS),
            _const_spec((1, d)),
            _const_spec((d, d)),
            _const_spec((d, kvw)),
            _const_spec((kvw, d)),
            _const_spec((LANES, tm // ATTN_PARTS)),
            _const_spec((1, LANES)),
            _const_spec((pairs, WINDOW, cols)),
            _const_spec((pairs, 1, cols)),
            _const_spec((d, d)),
        ],
        out_specs=_token_spec(tm),
        out_shape=jax.ShapeDtypeStruct(x.shape, _F32),
        scratch_shapes=[
            pltpu.VMEM((tm, d), _BF16),
            pltpu.VMEM((d, tm), _BF16),
            pltpu.VMEM((tm + WINDOW, kvw), _BF16),
            pltpu.VMEM((kvw, tm + WINDOW), _BF16),
            pltpu.VMEM((tm, d), _BF16),
        ],
        compiler_params=_params(),
        name="swa_attention",
    )(x, x, gain, w_qt, w_k, w_vt, q_gain, k_gain, bias, sink, w_o)


def kernel(x, conv_w_in, conv_w, conv_w_out, attn_w_qkv, attn_q_gain, attn_k_gain,
           attn_sinks, attn_w_o, norm_mixer, norm_ffn, ffn_w_gate_up, ffn_w_down):
    tm = TOKEN_TILE
    d = D_MODEL
    perm = np.asarray(HEAD_PERM)

    x, w_gu, w_down = _conv_mixer(
        x, norm_mixer[0][None, :], conv_w_in[0].astype(_BF16), conv_w[0],
        conv_w_out[0].astype(_BF16), ffn_w_gate_up, ffn_w_down, tm)
    x = _ffn(x, norm_ffn[0][None, :], w_gu, w_down, 0, tm)

    w_qkv = attn_w_qkv[0].astype(_BF16)
    w_qt = w_qkv[:, :d].reshape(d, N_Q_HEADS, HEAD_DIM)[:, perm].reshape(d, d).T
    kvw = N_KV_HEADS * HEAD_DIM
    w_k = w_qkv[:, d:d + kvw]
    w_vt = w_qkv[:, d + kvw:].T
    w_o = attn_w_o[0].astype(_BF16).reshape(N_Q_HEADS, HEAD_DIM, d)[perm].reshape(d, d)
    q_gain = jnp.broadcast_to((jnp.tile(attn_q_gain[0], 2) * LOG2E)[:, None],
                              (LANES, tm // ATTN_PARTS))
    k_gain = jnp.tile(attn_k_gain[0], 2)[None, :] * HEAD_DIM ** 0.5
    bias, sink = _attn_tables(attn_sinks[0])
    x = _attn_mixer(x, norm_mixer[1][None, :], w_qt, w_k, w_vt, q_gain, k_gain, bias, sink,
                    w_o, tm)
    x = _ffn(x, norm_ffn[1][None, :], w_gu, w_down, 1, tm)
    return x
```

```python
import functools

import numpy as np
import jax
import jax.numpy as jnp
from jax import lax
from jax.experimental import pallas as pl
from jax.experimental.pallas import tpu as pltpu

D_MODEL = 1024
HEAD_DIM = 64
N_Q_HEADS = 16
N_KV_HEADS = 4
GROUP = N_Q_HEADS // N_KV_HEADS
WINDOW = 128
CONV_WIDTH = 3
D_FF = 2816
EPS = 1e-6

LANES = 128
SUBLANES = 8
BF16_SUBLANES = 16
TOKEN_TILE = 1024
CONV_CHUNK = 256
FFN_CHUNK = 256
VMEM_LIMIT_BYTES = 56 * 1024 * 1024
MASK_VALUE = -1e30
LOG2E = 1.4426950408889634
ATTN_PARTS = 4
ATTN_COL_CHUNK = 256
ATTN_SKEW = 4
ATTN_PROJ_OFFSETS = (0, 3, 6, 10, 14)
ATTN_OUT_OFFSETS = (9, 13, 18, 22)

HEAD_PERM = (0, 4, 1, 5, 2, 6, 3, 7, 8, 12, 9, 13, 10, 14, 11, 15)

_F32 = jnp.float32
_BF16 = jnp.bfloat16


def _rmsnorm_bf16(x, gain):
    ms = jnp.mean(x * x, axis=-1, keepdims=True)
    return ((x * lax.rsqrt(ms + EPS)) * gain).astype(_BF16)


def _const_spec(shape):
    return pl.BlockSpec(shape, lambda b, s: (0,) * len(shape),
                        pipeline_mode=pl.Buffered(1))


def _token_spec(tm):
    return pl.BlockSpec((None, tm, D_MODEL), lambda b, s: (b, s, 0))


def _next_token_spec(tm, bsz, n_seq_tiles, rows=None):
    rows = tm if rows is None else rows

    def index(b, s):
        nxt = jnp.minimum(b * n_seq_tiles + s + 1, bsz * n_seq_tiles - 1)
        return (nxt // n_seq_tiles, (nxt % n_seq_tiles) * (tm // rows), 0)
    return pl.BlockSpec((None, rows, D_MODEL), index)


def _is_first_step():
    return jnp.logical_and(pl.program_id(0) == 0, pl.program_id(1) == 0)


def _params():
    return pltpu.CompilerParams(
        dimension_semantics=("arbitrary", "arbitrary"),
        vmem_limit_bytes=VMEM_LIMIT_BYTES)


def _cast_rows_per_step(n_rows, n_steps):
    rows = BF16_SUBLANES
    while n_rows % rows or n_rows // rows > n_steps:
        rows += BF16_SUBLANES
    return rows


def _cast_spec(shape, n_steps, n_seq_tiles):
    layers, n_rows, n_cols = shape
    rows = _cast_rows_per_step(n_rows, n_steps)
    last = n_rows // rows - 1

    def index(b, s):
        return (0, jnp.minimum(b * n_seq_tiles + s, last), 0)
    return pl.BlockSpec((layers, rows, n_cols), index)


def _conv_kernel(x_ref, g_ref, win_ref, cw_ref, wout_ref, wgu_ref, wdn_ref,
                 o_ref, wgu_bf_ref, wdn_bf_ref, h_scr, u_scr, z_scr, *, tm):
    cw = CONV_CHUNK
    wgu_bf_ref[...] = wgu_ref[...].astype(_BF16)
    wdn_bf_ref[...] = wdn_ref[...].astype(_BF16)

    @pl.when(pl.program_id(1) == 0)
    def _():
        u_scr[0:SUBLANES, :] = jnp.zeros((SUBLANES, D_MODEL), _F32)

    h_scr[...] = _rmsnorm_bf16(x_ref[...], g_ref[...])
    for j in range(D_MODEL // cw):
        cols = slice(j * cw, (j + 1) * cw)
        b_gate, c_gate, xv = [
            jnp.dot(h_scr[...], win_ref[:, part * D_MODEL + j * cw:part * D_MODEL + (j + 1) * cw],
                    preferred_element_type=_F32)
            for part in range(3)]
        u = b_gate * xv
        u_scr[SUBLANES:SUBLANES + tm, cols] = u
        w = cw_ref[:, cols]
        y = (w[2:3] * u
             + w[1:2] * u_scr[SUBLANES - 1:SUBLANES - 1 + tm, cols]
             + w[0:1] * u_scr[SUBLANES - 2:SUBLANES - 2 + tm, cols])
        z_scr[:, cols] = (c_gate * y).astype(_BF16)
        u_scr[0:SUBLANES, cols] = u_scr[tm:tm + SUBLANES, cols]
    mix = jnp.dot(z_scr[...], wout_ref[...], preferred_element_type=_F32)
    o_ref[...] = x_ref[...] + mix


def _conv_mixer(x, gain, w_in, conv_w, w_out, ffn_w_gate_up, ffn_w_down, tm):
    bsz, seq, d = x.shape
    n_seq_tiles = seq // tm
    n_steps = bsz * n_seq_tiles
    kern = functools.partial(_conv_kernel, tm=tm)
    return pl.pallas_call(
        kern,
        grid=(bsz, n_seq_tiles),
        in_specs=[
            _token_spec(tm),
            _const_spec((1, d)),
            _const_spec((d, 3 * d)),
            _const_spec((CONV_WIDTH, d)),
            _const_spec((d, d)),
            _cast_spec(ffn_w_gate_up.shape, n_steps, n_seq_tiles),
            _cast_spec(ffn_w_down.shape, n_steps, n_seq_tiles),
        ],
        out_specs=[
            _token_spec(tm),
            _cast_spec(ffn_w_gate_up.shape, n_steps, n_seq_tiles),
            _cast_spec(ffn_w_down.shape, n_steps, n_seq_tiles),
        ],
        out_shape=[
            jax.ShapeDtypeStruct(x.shape, _F32),
            jax.ShapeDtypeStruct(ffn_w_gate_up.shape, _BF16),
            jax.ShapeDtypeStruct(ffn_w_down.shape, _BF16),
        ],
        scratch_shapes=[
            pltpu.VMEM((tm, d), _BF16),
            pltpu.VMEM((tm + SUBLANES, d), _F32),
            pltpu.VMEM((tm, d), _BF16),
        ],
        compiler_params=_params(),
        name="conv_mixer",
    )(x, gain, w_in, conv_w, w_out, ffn_w_gate_up, ffn_w_down)


def _ffn_kernel(x_ref, xn_ref, g_ref, wgu_ref, wd_ref, o_ref, h_scr, a_scr):
    ck = FFN_CHUNK

    def gate_up_chunk(c):
        g = jnp.dot(h_scr[...], wgu_ref[:, c * ck:(c + 1) * ck],
                    preferred_element_type=_F32)
        u = jnp.dot(h_scr[...], wgu_ref[:, D_FF + c * ck:D_FF + (c + 1) * ck],
                    preferred_element_type=_F32)
        a_scr[:, c * ck:(c + 1) * ck] = (g * jax.nn.sigmoid(g) * u).astype(_BF16)

    @pl.when(_is_first_step())
    def _():
        h_scr[...] = _rmsnorm_bf16(x_ref[...], g_ref[...])
        gate_up_chunk(0)

    for c in range(1, D_FF // ck):
        gate_up_chunk(c)
    o_ref[...] = x_ref[...] + jnp.dot(a_scr[...], wd_ref[...],
                                      preferred_element_type=_F32)
    h_scr[...] = _rmsnorm_bf16(xn_ref[...], g_ref[...])
    gate_up_chunk(0)


def _layer_spec(shape, layer):
    return pl.BlockSpec((None,) + tuple(shape[1:]), lambda b, s: (layer, 0, 0),
                        pipeline_mode=pl.Buffered(1))


def _ffn(x, gain, w_gu, w_down, layer, tm):
    bsz, seq, d = x.shape
    return pl.pallas_call(
        _ffn_kernel,
        grid=(bsz, seq // tm),
        in_specs=[
            _token_spec(tm),
            _next_token_spec(tm, bsz, seq // tm),
            _const_spec((1, d)),
            _layer_spec(w_gu.shape, layer),
            _layer_spec(w_down.shape, layer),
        ],
        out_specs=_token_spec(tm),
        out_shape=jax.ShapeDtypeStruct(x.shape, _F32),
        scratch_shapes=[
            pltpu.VMEM((tm, d), _BF16),
            pltpu.VMEM((tm, D_FF), _BF16),
        ],
        compiler_params=_params(),
        name="swiglu_ffn",
    )(x, x, gain, w_gu, w_down)


def _pair_rmsnorm(t, gain):
    lo = lax.broadcasted_iota(jnp.int32, t.shape, 1) < HEAD_DIM
    sq = t * t
    s_lo = jnp.sum(jnp.where(lo, sq, 0.0), axis=-1, keepdims=True)
    s_hi = jnp.sum(jnp.where(lo, 0.0, sq), axis=-1, keepdims=True)
    r = lax.rsqrt(jnp.where(lo, s_lo, s_hi) + HEAD_DIM * EPS)
    return ((t * r) * gain).astype(_BF16)


def _pair_rmsnorm_t(t, gain):
    halves = []
    for hd in range(2):
        th = t[hd * HEAD_DIM:(hd + 1) * HEAD_DIM]
        ss = jnp.sum(th * th, axis=0, keepdims=True)
        halves.append(th * lax.rsqrt(ss + HEAD_DIM * EPS))
    return (jnp.concatenate(halves, axis=0) * gain).astype(_BF16)


def _attn_kernel(x_ref, xn_ref, g_ref, wqt_ref, wkv_ref, qg_ref, kg_ref, bias_ref, sink_ref,
                 wo_ref, o_ref, h_scr, qt_scr, k_scr, v_scr, a_scr, *, tm):
    blk = WINDOW
    d = D_MODEL
    kvw = N_KV_HEADS * HEAD_DIM
    part_rows = tm // ATTN_PARTS
    ncol = ATTN_COL_CHUNK
    seq_tile = pl.program_id(1)

    @pl.when(seq_tile == 0)
    def _():
        k_scr[0:blk, :] = jnp.zeros((blk, kvw), _BF16)
        v_scr[0:blk, :] = jnp.zeros((blk, kvw), _BF16)

    def project_pieces(part, src_ref):
        r0 = part * part_rows
        rows = slice(r0, r0 + part_rows)

        def q_piece(c):
            if c == 0:
                h_scr[rows, :] = _rmsnorm_bf16(src_ref[rows, :], g_ref[...])
            qt = lax.dot_general(wqt_ref[c * ncol:(c + 1) * ncol, :], h_scr[rows, :],
                                 (((1,), (1,)), ((), ())),
                                 preferred_element_type=_F32)
            for j in range(ncol // LANES):
                dims = slice(c * ncol + j * LANES, c * ncol + (j + 1) * LANES)
                qt_scr[dims, rows] = _pair_rmsnorm_t(qt[j * LANES:(j + 1) * LANES],
                                                     qg_ref[...])

        def kv_piece():
            kv = jnp.dot(h_scr[rows, :], wkv_ref[...], preferred_element_type=_F32)
            for p in range(kvw // LANES):
                k_scr[blk + r0:blk + r0 + part_rows, p * LANES:(p + 1) * LANES] = (
                    _pair_rmsnorm(kv[:, p * LANES:(p + 1) * LANES], kg_ref[...]))
            v_scr[blk + r0:blk + r0 + part_rows, :] = kv[:, kvw:].astype(_BF16)

        return [functools.partial(q_piece, c) for c in range(d // ncol)] + [kv_piece]

    def output_pieces(part):
        rows = slice(part * part_rows, (part + 1) * part_rows)

        def o_piece(c):
            lanes = slice(c * ncol, (c + 1) * ncol)
            o_ref[rows, lanes] = x_ref[rows, lanes] + jnp.dot(
                a_scr[rows, :], wo_ref[:, lanes], preferred_element_type=_F32)

        return [functools.partial(o_piece, c) for c in range(d // ncol)]

    @pl.when(_is_first_step())
    def _():
        for piece in project_pieces(0, x_ref):
            piece()

    lo_row = lax.broadcasted_iota(jnp.int32, (LANES, blk), 0) < HEAD_DIM
    cols = 2 * blk
    kc = lax.broadcasted_iota(jnp.int32, (blk, cols), 0)
    qi = lax.broadcasted_iota(jnp.int32, (blk, cols), 1) % blk
    in_cur = kc <= qi
    first_pen = jnp.where(seq_tile == 0, MASK_VALUE, 0.0)

    def scores(qb, pb):
        r0, p = qb * blk, pb // GROUP
        qp = qt_scr[pb * LANES:(pb + 1) * LANES, r0:r0 + blk]
        zq = jnp.zeros_like(qp)
        qm = jnp.concatenate([jnp.where(lo_row, qp, zq),
                              jnp.where(lo_row, zq, qp)], axis=1)
        kw = k_scr[r0:r0 + 2 * blk, p * LANES:(p + 1) * LANES]
        s2 = jnp.dot(kw, qm, preferred_element_type=_F32)
        s_prev = s2[:blk] + first_pen if qb == 0 else s2[:blk]
        return jnp.where(in_cur, s2[blk:], s_prev)

    def softmax(qb, pb, s_pair):
        halves = []
        for hd in range(2):
            lanes = slice(hd * blk, (hd + 1) * blk)
            s = s_pair[:, lanes] + bias_ref[pb, :, lanes]
            sink = sink_ref[pb, :, lanes]
            m = jnp.maximum(jnp.max(s, axis=0, keepdims=True), sink)
            e = jnp.exp2(s - m)
            denom = jnp.sum(e, axis=0, keepdims=True) + jnp.exp2(sink - m)
            pr = (e * (1.0 / denom)).astype(_BF16)
            zero = jnp.zeros_like(pr)
            cur = in_cur[:, lanes]
            halves.append(jnp.concatenate([jnp.where(cur, zero, pr),
                                           jnp.where(cur, pr, zero)], axis=0))
        return jnp.concatenate(halves, axis=1)

    def weighted_values(qb, pb, p2):
        r0, p = qb * blk, pb // GROUP
        vw = v_scr[r0:r0 + 2 * blk, p * LANES:(p + 1) * LANES]
        o = lax.dot_general(vw, p2, (((0,), (0,)), ((), ())),
                            preferred_element_type=_F32)
        oj = jnp.where(lo_row, o[:, :blk], o[:, blk:])
        a_scr[r0:r0 + blk, pb * LANES:(pb + 1) * LANES] = oj.T.astype(_BF16)

    chains = [(qb, pb) for qb in range(tm // blk) for pb in range(N_Q_HEADS // 2)]
    part_chains = len(chains) // ATTN_PARTS
    extra = {}
    for part in range(1, ATTN_PARTS + 1):
        pieces = (project_pieces(part, x_ref) if part < ATTN_PARTS
                  else project_pieces(0, xn_ref))
        base = (part - 1) * part_chains
        for offset, piece in zip(ATTN_PROJ_OFFSETS, pieces, strict=True):
            assert offset < part_chains
            extra.setdefault(base + offset, []).append(piece)
    for part in range(ATTN_PARTS - 1):
        pieces = output_pieces(part)
        base = (part + 1) * part_chains
        for offset, piece in zip(ATTN_OUT_OFFSETS, pieces, strict=True):
            assert 2 * ATTN_SKEW < offset and base + offset < len(chains) + 2 * ATTN_SKEW
            extra.setdefault(base + offset, []).append(piece)

    s2_vals, p2_vals = {}, {}
    for step in range(len(chains) + 2 * ATTN_SKEW):
        if step < len(chains):
            s2_vals[step] = scores(*chains[step])
        for piece in extra.get(step, ()):
            piece()
        i = step - ATTN_SKEW
        if 0 <= i < len(chains):
            p2_vals[i] = softmax(*chains[i], s2_vals.pop(i))
        i = step - 2 * ATTN_SKEW
        if 0 <= i < len(chains):
            weighted_values(*chains[i], p2_vals.pop(i))

    for piece in output_pieces(ATTN_PARTS - 1):
        piece()
    k_scr[0:blk, :] = k_scr[tm:tm + blk, :]
    v_scr[0:blk, :] = v_scr[tm:tm + blk, :]


def _attn_tables(sinks):
    perm = np.asarray(HEAD_PERM)
    slopes = np.exp2(-8.0 * np.arange(1, N_Q_HEADS + 1, dtype=np.float64) / N_Q_HEADS)
    kc = np.arange(WINDOW)[:, None]
    qi = np.arange(WINDOW)[None, :]
    dist = ((qi - kc) % WINDOW).astype(np.float64)
    bias = -slopes[perm][:, None, None] * dist[None] * LOG2E
    bias = bias.reshape(N_Q_HEADS // 2, 2, WINDOW, WINDOW).transpose(0, 2, 1, 3)
    bias = jnp.asarray(bias.reshape(N_Q_HEADS // 2, WINDOW, 2 * WINDOW), _F32)
    sink = jnp.repeat(sinks.astype(_F32)[perm] * LOG2E, WINDOW)
    return bias, sink.reshape(N_Q_HEADS // 2, 1, 2 * WINDOW)


def _attn_mixer(x, gain, w_qt, w_kv, q_gain, k_gain, bias, sink, w_o, tm):
    bsz, seq, d = x.shape
    kvw = N_KV_HEADS * HEAD_DIM
    pairs = N_Q_HEADS // 2
    cols = 2 * WINDOW
    kern = functools.partial(_attn_kernel, tm=tm)
    return pl.pallas_call(
        kern,
        grid=(bsz, seq // tm),
        in_specs=[
            _token_spec(tm),
            _next_token_spec(tm, bsz, seq // tm, rows=tm // ATTN_PARTS),
            _const_spec((1, d)),
            _const_spec((d, d)),
            _const_spec((d, 2 * kvw)),
            _const_spec((LANES, tm // ATTN_PARTS)),
            _const_spec((1, LANES)),
            _const_spec((pairs, WINDOW, cols)),
            _const_spec((pairs, 1, cols)),
            _const_spec((d, d)),
        ],
        out_specs=_token_spec(tm),
        out_shape=jax.ShapeDtypeStruct(x.shape, _F32),
        scratch_shapes=[
            pltpu.VMEM((tm, d), _BF16),
            pltpu.VMEM((d, tm), _BF16),
            pltpu.VMEM((tm + WINDOW, kvw), _BF16),
            pltpu.VMEM((tm + WINDOW, kvw), _BF16),
            pltpu.VMEM((tm, d), _BF16),
        ],
        compiler_params=_params(),
        name="swa_attention",
    )(x, x, gain, w_qt, w_kv, q_gain, k_gain, bias, sink, w_o)


def kernel(x, conv_w_in, conv_w, conv_w_out, attn_w_qkv, attn_q_gain, attn_k_gain,
           attn_sinks, attn_w_o, norm_mixer, norm_ffn, ffn_w_gate_up, ffn_w_down):
    tm = TOKEN_TILE
    d = D_MODEL
    perm = np.asarray(HEAD_PERM)

    x, w_gu, w_down = _conv_mixer(
        x, norm_mixer[0][None, :], conv_w_in[0].astype(_BF16), conv_w[0],
        conv_w_out[0].astype(_BF16), ffn_w_gate_up, ffn_w_down, tm)
    x = _ffn(x, norm_ffn[0][None, :], w_gu, w_down, 0, tm)

    w_qkv = attn_w_qkv[0].astype(_BF16)
    w_qt = w_qkv[:, :d].reshape(d, N_Q_HEADS, HEAD_DIM)[:, perm].reshape(d, d).T
    w_kv = w_qkv[:, d:]
    w_o = attn_w_o[0].astype(_BF16).reshape(N_Q_HEADS, HEAD_DIM, d)[perm].reshape(d, d)
    q_gain = jnp.broadcast_to((jnp.tile(attn_q_gain[0], 2) * LOG2E)[:, None],
                              (LANES, tm // ATTN_PARTS))
    k_gain = jnp.tile(attn_k_gain[0], 2)[None, :] * HEAD_DIM ** 0.5
    bias, sink = _attn_tables(attn_sinks[0])
    x = _attn_mixer(x, norm_mixer[1][None, :], w_qt, w_kv, q_gain, k_gain, bias, sink, w_o,
                    tm)
    x = _ffn(x, norm_ffn[1][None, :], w_gu, w_down, 1, tm)
    return x
```
